```python
import jax, jax.numpy as jnp
from jax import lax
import numpy as np

D_MODEL = 2048
BATCH = 8
SEQ = 8192
DEPTH = 4

N_MIXERS = 3
N_A = (DEPTH + 2) // 3
N_B = (DEPTH + 1) // 3
N_C = DEPTH // 3
POOL_WINDOWS = (2, 4, 8, 16)
N_POOL_GROUPS = len(POOL_WINDOWS)
POOL_GROUP = D_MODEL // N_POOL_GROUPS
CONV_WIDTH = 3
HEAD_DIM = 128
N_HEADS = D_MODEL // HEAD_DIM
Q_BLOCK = 128
D_FF = -(-8 * D_MODEL // (3 * 256)) * 256
N_MOD = 6
EPS = 1e-6

kernel_name = "hybrid_pool_conv_stickbreak_adaln"


def rmsnorm(x, g):
    xf = x.astype(jnp.float32)
    ms = jnp.mean(xf * xf, axis=-1, keepdims=True)
    return (xf * lax.rsqrt(ms + EPS)).astype(x.dtype) * g


def modulate(h, shift, scale):
    return h * (1 + scale[:, None, :]) + shift[:, None, :]


def causal_window_mean(u, w):
    S = u.shape[1]
    cs = lax.cumsum(u, axis=1)
    cs_shift = jnp.pad(cs, ((0, 0), (w, 0), (0, 0)))[:, :S]
    count = jnp.minimum(jnp.arange(1, S + 1), w).astype(jnp.float32)
    return (cs - cs_shift) / count[None, :, None]


def pool_mixer(h, w_pool, pool_scale):
    B, S, D = h.shape
    hf = h.reshape(B, S, N_POOL_GROUPS, POOL_GROUP).astype(jnp.float32)
    pooled = jnp.stack(
        [causal_window_mean(hf[:, :, g], w) for g, w in enumerate(POOL_WINDOWS)], axis=2)
    diff = (pooled - hf).astype(h.dtype)
    y = jnp.einsum('bsgc,gcd->bsgd', diff, w_pool).reshape(B, S, D)
    return y * pool_scale


def conv_mixer(h, w_in, conv_w, w_out):
    u = h @ w_in
    b_gate, c_gate, v = jnp.split(u, 3, axis=-1)
    z = c_gate * v
    zc = lax.conv_general_dilated(
        z, conv_w, window_strides=(1,), padding=((CONV_WIDTH - 1, 0),),
        dimension_numbers=('NWC', 'WIO', 'NWC'), feature_group_count=z.shape[-1])
    return (b_gate * zc) @ w_out


def stick_breaking_attention(q, k, v):
    B, H, S, Dh = q.shape
    nb = S // Q_BLOCK
    qb = q.reshape(B, H, nb, Q_BLOCK, Dh).transpose(2, 0, 1, 3, 4)
    key_pos = jnp.arange(S)
    scale = Dh ** -0.5

    def block(args):
        q_blk, i = args
        z = jnp.einsum('bhqd,bhkd->bhqk', q_blk, k,
                       preferred_element_type=jnp.float32) * scale
        q_pos = i * Q_BLOCK + jnp.arange(Q_BLOCK)
        causal = key_pos[None, :] < q_pos[:, None]
        log_beta = jax.nn.log_sigmoid(z)
        log_1m = jnp.where(causal, jax.nn.log_sigmoid(-z), 0.0)
        suffix = lax.cumsum(log_1m, axis=3, reverse=True) - log_1m
        a = jnp.where(causal, jnp.exp(log_beta + suffix), 0.0)
        return jnp.einsum('bhqk,bhkd->bhqd', a.astype(v.dtype), v)

    o = lax.map(block, (qb, jnp.arange(nb)))
    return o.transpose(1, 2, 0, 3, 4).reshape(B, H, S, Dh)


def sb_mixer(h, w_qkv, w_o):
    B, S, D = h.shape
    qkv = (h @ w_qkv).reshape(B, S, 3, N_HEADS, HEAD_DIM)
    q = qkv[:, :, 0].transpose(0, 2, 1, 3)
    k = qkv[:, :, 1].transpose(0, 2, 1, 3)
    v = qkv[:, :, 2].transpose(0, 2, 1, 3)
    o = stick_breaking_attention(q, k, v)
    return o.transpose(0, 2, 1, 3).reshape(B, S, D) @ w_o


def swiglu(h, w_gate, w_up, w_down):
    return (jax.nn.silu(h @ w_gate) * (h @ w_up)) @ w_down


def _fwd_setup_inputs(seed: int = 0) -> dict:
    key = jax.random.key(seed)
    ks = jax.random.split(key, 20)
    D = D_MODEL
    nrm = jax.random.normal
    f32 = jnp.float32
    return {
        "x": nrm(ks[0], (BATCH, SEQ, D), f32),
        "c": nrm(ks[1], (BATCH, D), f32),
        "norm_mix_g": 1.0 + 0.05 * nrm(ks[2], (DEPTH, D), f32),
        "norm_ffn_g": 1.0 + 0.05 * nrm(ks[3], (DEPTH, D), f32),
        "w_mod": 0.5 * D ** -0.5 * nrm(ks[4], (DEPTH, D, N_MOD * D), f32),
        "b_mod": 0.1 * nrm(ks[5], (DEPTH, N_MOD * D), f32),
        "pool_w": POOL_GROUP ** -0.5 * nrm(ks[6], (N_A, N_POOL_GROUPS, POOL_GROUP, POOL_GROUP), f32),
        "pool_scale": 1.0 + 0.1 * nrm(ks[7], (N_A, D), f32),
        "conv_w_in": D ** -0.5 * nrm(ks[8], (N_B, D, 3 * D), f32),
        "conv_w": CONV_WIDTH ** -0.5 * nrm(ks[9], (N_B, CONV_WIDTH, 1, D), f32),
        "conv_w_out": D ** -0.5 * nrm(ks[10], (N_B, D, D), f32),
        "sb_w_qkv": D ** -0.5 * nrm(ks[11], (N_C, D, 3 * D), f32),
        "sb_w_o": D ** -0.5 * nrm(ks[12], (N_C, D, D), f32),
        "ffn_w_gate": D ** -0.5 * nrm(ks[13], (DEPTH, D, D_FF), f32),
        "ffn_w_up": D ** -0.5 * nrm(ks[14], (DEPTH, D, D_FF), f32),
        "ffn_w_down": D_FF ** -0.5 * nrm(ks[15], (DEPTH, D_FF, D), f32),
        "final_g": 1.0 + 0.05 * nrm(ks[16], (D,), f32),
    }


def _fwd_reference(x, c, norm_mix_g, norm_ffn_g, w_mod, b_mod, pool_w, pool_scale,
              conv_w_in, conv_w, conv_w_out, sb_w_qkv, sb_w_o,
              ffn_w_gate, ffn_w_up, ffn_w_down, final_g):
    mod = jnp.einsum('bd,lde->lbe', jax.nn.silu(c), w_mod) + b_mod[:, None, :]
    h = x
    for i in range(DEPTH):
        shift_m, scale_m, gate_m, shift_f, scale_f, gate_f = jnp.split(mod[i], N_MOD, axis=-1)
        u = modulate(rmsnorm(h, norm_mix_g[i]), shift_m, scale_m)
        kind, j = i % N_MIXERS, i // N_MIXERS
        if kind == 0:
            y = pool_mixer(u, pool_w[j], pool_scale[j])
        elif kind == 1:
            y = conv_mixer(u, conv_w_in[j], conv_w[j], conv_w_out[j])
        else:
            y = sb_mixer(u, sb_w_qkv[j], sb_w_o[j])
        h = h + gate_m[:, None, :] * y
        u = modulate(rmsnorm(h, norm_ffn_g[i]), shift_f, scale_f)
        h = h + gate_f[:, None, :] * swiglu(u, ffn_w_gate[i], ffn_w_up[i], ffn_w_down[i])
    return rmsnorm(h, final_g)


import jax as _jax
import jax.numpy as _jnp

TWIN_FORMAT = 'train_step'
FWD_PARAMS = ['x', 'c', 'norm_mix_g', 'norm_ffn_g', 'w_mod', 'b_mod', 'pool_w', 'pool_scale', 'conv_w_in', 'conv_w', 'conv_w_out', 'sb_w_qkv', 'sb_w_o', 'ffn_w_gate', 'ffn_w_up', 'ffn_w_down', 'final_g']
TWIN_WEIGHTS = ['norm_mix_g', 'norm_ffn_g', 'w_mod', 'b_mod', 'pool_w', 'pool_scale', 'conv_w_in', 'conv_w', 'conv_w_out', 'sb_w_qkv', 'sb_w_o', 'ffn_w_gate', 'ffn_w_up', 'ffn_w_down', 'final_g']
TWIN_DIFF_INPUT = 'x'
TWIN_INPUTS = ['x', 'c', 'norm_mix_g', 'norm_ffn_g', 'w_mod', 'b_mod', 'pool_w', 'pool_scale', 'conv_w_in', 'conv_w', 'conv_w_out', 'sb_w_qkv', 'sb_w_o', 'ffn_w_gate', 'ffn_w_up', 'ffn_w_down', 'final_g', 'loss_target', 'm_norm_mix_g', 'm_norm_ffn_g', 'm_w_mod', 'm_b_mod', 'm_pool_w', 'm_pool_scale', 'm_conv_w_in', 'm_conv_w', 'm_conv_w_out', 'm_sb_w_qkv', 'm_sb_w_o', 'm_ffn_w_gate', 'm_ffn_w_up', 'm_ffn_w_down', 'm_final_g', 'v_norm_mix_g', 'v_norm_ffn_g', 'v_w_mod', 'v_b_mod', 'v_pool_w', 'v_pool_scale', 'v_conv_w_in', 'v_conv_w', 'v_conv_w_out', 'v_sb_w_qkv', 'v_sb_w_o', 'v_ffn_w_gate', 'v_ffn_w_up', 'v_ffn_w_down', 'v_final_g']
TWIN_OUTPUTS = ['loss', 'grad_x', 'grad_norm_mix_g', 'grad_norm_ffn_g', 'grad_w_mod', 'grad_b_mod', 'grad_pool_w', 'grad_pool_scale', 'grad_conv_w_in', 'grad_conv_w', 'grad_conv_w_out', 'grad_sb_w_qkv', 'grad_sb_w_o', 'grad_ffn_w_gate', 'grad_ffn_w_up', 'grad_ffn_w_down', 'grad_final_g', 'delta_norm_mix_g', 'delta_norm_ffn_g', 'delta_w_mod', 'delta_b_mod', 'delta_pool_w', 'delta_pool_scale', 'delta_conv_w_in', 'delta_conv_w', 'delta_conv_w_out', 'delta_sb_w_qkv', 'delta_sb_w_o', 'delta_ffn_w_gate', 'delta_ffn_w_up', 'delta_ffn_w_down', 'delta_final_g', 'new_m_norm_mix_g', 'new_m_norm_ffn_g', 'new_m_w_mod', 'new_m_b_mod', 'new_m_pool_w', 'new_m_pool_scale', 'new_m_conv_w_in', 'new_m_conv_w', 'new_m_conv_w_out', 'new_m_sb_w_qkv', 'new_m_sb_w_o', 'new_m_ffn_w_gate', 'new_m_ffn_w_up', 'new_m_ffn_w_down', 'new_m_final_g', 'new_v_norm_mix_g', 'new_v_norm_ffn_g', 'new_v_w_mod', 'new_v_b_mod', 'new_v_pool_w', 'new_v_pool_scale', 'new_v_conv_w_in', 'new_v_conv_w', 'new_v_conv_w_out', 'new_v_sb_w_qkv', 'new_v_sb_w_o', 'new_v_ffn_w_gate', 'new_v_ffn_w_up', 'new_v_ffn_w_down', 'new_v_final_g']
TWIN_LEAF_KINDS = {'loss': 'loss', 'grad_x': 'grad_x', 'grad_norm_mix_g': 'grad_w', 'grad_norm_ffn_g': 'grad_w', 'grad_w_mod': 'grad_w', 'grad_b_mod': 'grad_w', 'grad_pool_w': 'grad_w', 'grad_pool_scale': 'grad_w', 'grad_conv_w_in': 'grad_w', 'grad_conv_w': 'grad_w', 'grad_conv_w_out': 'grad_w', 'grad_sb_w_qkv': 'grad_w', 'grad_sb_w_o': 'grad_w', 'grad_ffn_w_gate': 'grad_w', 'grad_ffn_w_up': 'grad_w', 'grad_ffn_w_down': 'grad_w', 'grad_final_g': 'grad_w', 'delta_norm_mix_g': 'delta_w', 'delta_norm_ffn_g': 'delta_w', 'delta_w_mod': 'delta_w', 'delta_b_mod': 'delta_w', 'delta_pool_w': 'delta_w', 'delta_pool_scale': 'delta_w', 'delta_conv_w_in': 'delta_w', 'delta_conv_w': 'delta_w', 'delta_conv_w_out': 'delta_w', 'delta_sb_w_qkv': 'delta_w', 'delta_sb_w_o': 'delta_w', 'delta_ffn_w_gate': 'delta_w', 'delta_ffn_w_up': 'delta_w', 'delta_ffn_w_down': 'delta_w', 'delta_final_g': 'delta_w', 'new_m_norm_mix_g': 'new_m', 'new_m_norm_ffn_g': 'new_m', 'new_m_w_mod': 'new_m', 'new_m_b_mod': 'new_m', 'new_m_pool_w': 'new_m', 'new_m_pool_scale': 'new_m', 'new_m_conv_w_in': 'new_m', 'new_m_conv_w': 'new_m', 'new_m_conv_w_out': 'new_m', 'new_m_sb_w_qkv': 'new_m', 'new_m_sb_w_o': 'new_m', 'new_m_ffn_w_gate': 'new_m', 'new_m_ffn_w_up': 'new_m', 'new_m_ffn_w_down': 'new_m', 'new_m_final_g': 'new_m', 'new_v_norm_mix_g': 'new_v', 'new_v_norm_ffn_g': 'new_v', 'new_v_w_mod': 'new_v', 'new_v_b_mod': 'new_v', 'new_v_pool_w': 'new_v', 'new_v_pool_scale': 'new_v', 'new_v_conv_w_in': 'new_v', 'new_v_conv_w': 'new_v', 'new_v_conv_w_out': 'new_v', 'new_v_sb_w_qkv': 'new_v', 'new_v_sb_w_o': 'new_v', 'new_v_ffn_w_gate': 'new_v', 'new_v_ffn_w_up': 'new_v', 'new_v_ffn_w_down': 'new_v', 'new_v_final_g': 'new_v'}


def _forward(args):
    return _fwd_reference(*[args[k] for k in FWD_PARAMS])


def _output_shape():
    def fwd():
        inp = _fwd_setup_inputs(0)
        return _fwd_reference(*[inp[k] for k in FWD_PARAMS])
    out = _jax.eval_shape(fwd)
    return out.shape, out.dtype

N_MICROBATCH = 1
ADAM_LR = 0.001
ADAM_B1 = 0.9
ADAM_B2 = 0.999
ADAM_EPS = 1e-08
ADAM_WD = 0.01
ADAM_STEP = 10
PER_EXAMPLE_BATCH_AXIS = {'x': 0, 'c': 0, 'loss_target': 0}
SHARED_INPUTS = []
_WEIGHT_DTYPES = {'norm_mix_g': _jnp.float32, 'norm_ffn_g': _jnp.float32, 'w_mod': _jnp.float32, 'b_mod': _jnp.float32, 'pool_w': _jnp.float32, 'pool_scale': _jnp.float32, 'conv_w_in': _jnp.float32, 'conv_w': _jnp.float32, 'conv_w_out': _jnp.float32, 'sb_w_qkv': _jnp.float32, 'sb_w_o': _jnp.float32, 'ffn_w_gate': _jnp.float32, 'ffn_w_up': _jnp.float32, 'ffn_w_down': _jnp.float32, 'final_g': _jnp.float32}
MOMENT_SCALE = {'norm_mix_g': 5.205940e-02, 'norm_ffn_g': 3.737360e-02, 'w_mod': 5.856565e-02, 'b_mod': 1.297983e-01, 'pool_w': 3.749084e-02, 'pool_scale': 1.525468e-01, 'conv_w_in': 4.872785e-02, 'conv_w': 4.902166e-02, 'conv_w_out': 4.921138e-02, 'sb_w_qkv': 1.859161e-02, 'sb_w_o': 2.693203e-02, 'ffn_w_gate': 1.629235e-02, 'ffn_w_up': 1.577815e-02, 'ffn_w_down': 2.620250e-02, 'final_g': 3.215832e+01}


def _to_microbatches(a, axis):
    t = _jnp.moveaxis(a, axis, 0)
    t = t.reshape((N_MICROBATCH, t.shape[0] // N_MICROBATCH) + t.shape[1:])
    return _jnp.moveaxis(t, 1, axis + 1)


def setup_inputs(seed: int = 0) -> dict:
    inp = _fwd_setup_inputs(seed)
    key = _jax.random.fold_in(_jax.random.key(seed), 7919)
    shape, _ = _output_shape()
    out = dict(inp)
    out["loss_target"] = _jax.random.normal(_jax.random.fold_in(key, 0), shape, _jnp.float32)
    for i, name in enumerate(TWIN_WEIGHTS):
        w = inp[name].astype(_jnp.float32)
        if MOMENT_SCALE is None:
            s = _jnp.sqrt(_jnp.mean(_jnp.square(w)) + 1e-30)
        else:
            s = MOMENT_SCALE[name]
        km, kv = _jax.random.split(_jax.random.fold_in(key, i + 1))
        out[name] = w
        out["m_" + name] = s * _jax.random.normal(km, w.shape, _jnp.float32)
        out["v_" + name] = (s * s) * _jax.random.uniform(kv, w.shape, _jnp.float32, 0.5, 1.5)
    if N_MICROBATCH > 1:
        for name, axis in PER_EXAMPLE_BATCH_AXIS.items():
            out[name] = _to_microbatches(out[name], axis)
    return {'x': out['x'], 'c': out['c'], 'norm_mix_g': out['norm_mix_g'], 'norm_ffn_g': out['norm_ffn_g'], 'w_mod': out['w_mod'], 'b_mod': out['b_mod'], 'pool_w': out['pool_w'], 'pool_scale': out['pool_scale'], 'conv_w_in': out['conv_w_in'], 'conv_w': out['conv_w'], 'conv_w_out': out['conv_w_out'], 'sb_w_qkv': out['sb_w_qkv'], 'sb_w_o': out['sb_w_o'], 'ffn_w_gate': out['ffn_w_gate'], 'ffn_w_up': out['ffn_w_up'], 'ffn_w_down': out['ffn_w_down'], 'final_g': out['final_g'], 'loss_target': out['loss_target'], 'm_norm_mix_g': out['m_norm_mix_g'], 'm_norm_ffn_g': out['m_norm_ffn_g'], 'm_w_mod': out['m_w_mod'], 'm_b_mod': out['m_b_mod'], 'm_pool_w': out['m_pool_w'], 'm_pool_scale': out['m_pool_scale'], 'm_conv_w_in': out['m_conv_w_in'], 'm_conv_w': out['m_conv_w'], 'm_conv_w_out': out['m_conv_w_out'], 'm_sb_w_qkv': out['m_sb_w_qkv'], 'm_sb_w_o': out['m_sb_w_o'], 'm_ffn_w_gate': out['m_ffn_w_gate'], 'm_ffn_w_up': out['m_ffn_w_up'], 'm_ffn_w_down': out['m_ffn_w_down'], 'm_final_g': out['m_final_g'], 'v_norm_mix_g': out['v_norm_mix_g'], 'v_norm_ffn_g': out['v_norm_ffn_g'], 'v_w_mod': out['v_w_mod'], 'v_b_mod': out['v_b_mod'], 'v_pool_w': out['v_pool_w'], 'v_pool_scale': out['v_pool_scale'], 'v_conv_w_in': out['v_conv_w_in'], 'v_conv_w': out['v_conv_w'], 'v_conv_w_out': out['v_conv_w_out'], 'v_sb_w_qkv': out['v_sb_w_qkv'], 'v_sb_w_o': out['v_sb_w_o'], 'v_ffn_w_gate': out['v_ffn_w_gate'], 'v_ffn_w_up': out['v_ffn_w_up'], 'v_ffn_w_down': out['v_ffn_w_down'], 'v_final_g': out['v_final_g']}


def _loss(weights, diff, rest, loss_target):
    with _jax.named_scope("forward"):
        args = {**rest, TWIN_DIFF_INPUT: diff, **{k: w.astype(_WEIGHT_DTYPES[k]) for k, w in weights.items()}}
        y = _forward(args)
    with _jax.named_scope("loss_head"):
        err = _jnp.square(y.astype(_jnp.float32) - loss_target)
        return 0.5 * _jnp.sum(_jnp.mean(err, axis=-1)) if err.ndim else 0.5 * err


def _adamw(w, g, m, v):
    m = ADAM_B1 * m + (1.0 - ADAM_B1) * g
    v = ADAM_B2 * v + (1.0 - ADAM_B2) * _jnp.square(g)
    m_hat = m / (1.0 - ADAM_B1 ** ADAM_STEP)
    v_hat = v / (1.0 - ADAM_B2 ** ADAM_STEP)
    delta = -ADAM_LR * (m_hat / (_jnp.sqrt(v_hat) + ADAM_EPS) + ADAM_WD * w)
    return delta, m, v


def reference(x, c, norm_mix_g, norm_ffn_g, w_mod, b_mod, pool_w, pool_scale, conv_w_in, conv_w, conv_w_out, sb_w_qkv, sb_w_o, ffn_w_gate, ffn_w_up, ffn_w_down, final_g, loss_target, m_norm_mix_g, m_norm_ffn_g, m_w_mod, m_b_mod, m_pool_w, m_pool_scale, m_conv_w_in, m_conv_w, m_conv_w_out, m_sb_w_qkv, m_sb_w_o, m_ffn_w_gate, m_ffn_w_up, m_ffn_w_down, m_final_g, v_norm_mix_g, v_norm_ffn_g, v_w_mod, v_b_mod, v_pool_w, v_pool_scale, v_conv_w_in, v_conv_w, v_conv_w_out, v_sb_w_qkv, v_sb_w_o, v_ffn_w_gate, v_ffn_w_up, v_ffn_w_down, v_final_g):
    given = dict(x=x, c=c, norm_mix_g=norm_mix_g, norm_ffn_g=norm_ffn_g, w_mod=w_mod, b_mod=b_mod, pool_w=pool_w, pool_scale=pool_scale, conv_w_in=conv_w_in, conv_w=conv_w, conv_w_out=conv_w_out, sb_w_qkv=sb_w_qkv, sb_w_o=sb_w_o, ffn_w_gate=ffn_w_gate, ffn_w_up=ffn_w_up, ffn_w_down=ffn_w_down, final_g=final_g, loss_target=loss_target, m_norm_mix_g=m_norm_mix_g, m_norm_ffn_g=m_norm_ffn_g, m_w_mod=m_w_mod, m_b_mod=m_b_mod, m_pool_w=m_pool_w, m_pool_scale=m_pool_scale, m_conv_w_in=m_conv_w_in, m_conv_w=m_conv_w, m_conv_w_out=m_conv_w_out, m_sb_w_qkv=m_sb_w_qkv, m_sb_w_o=m_sb_w_o, m_ffn_w_gate=m_ffn_w_gate, m_ffn_w_up=m_ffn_w_up, m_ffn_w_down=m_ffn_w_down, m_final_g=m_final_g, v_norm_mix_g=v_norm_mix_g, v_norm_ffn_g=v_norm_ffn_g, v_w_mod=v_w_mod, v_b_mod=v_b_mod, v_pool_w=v_pool_w, v_pool_scale=v_pool_scale, v_conv_w_in=v_conv_w_in, v_conv_w=v_conv_w, v_conv_w_out=v_conv_w_out, v_sb_w_qkv=v_sb_w_qkv, v_sb_w_o=v_sb_w_o, v_ffn_w_gate=v_ffn_w_gate, v_ffn_w_up=v_ffn_w_up, v_ffn_w_down=v_ffn_w_down, v_final_g=v_final_g)
    weights = {n: given[n] for n in TWIN_WEIGHTS}
    shared = {n: given[n] for n in SHARED_INPUTS}
    per_example = {n: given[n] for n in ['x', 'c']}
    grad_fn = _jax.value_and_grad(_loss, argnums=(0, 1))

    def one_microbatch(ex, loss_target):
        ex = dict(ex)
        diff = ex.pop(TWIN_DIFF_INPUT)
        return grad_fn(weights, diff, {**shared, **ex}, loss_target)

    if N_MICROBATCH == 1:
        loss, (grad_w, grad_x) = one_microbatch(per_example, given["loss_target"])
    else:
        def body(carry, xs):
            loss_sum, grad_sum = carry
            l_k, (gw_k, gx_k) = one_microbatch(xs[0], xs[1])
            with _jax.named_scope("update"):
                return (loss_sum + l_k, _jax.tree.map(_jnp.add, grad_sum, gw_k)), gx_k

        init = (_jnp.zeros((), _jnp.float32), _jax.tree.map(_jnp.zeros_like, weights))
        (loss, grad_w), grad_x = _jax.lax.scan(body, init, (per_example, given["loss_target"]))
    with _jax.named_scope("update"):
        delta_w, new_m, new_v = {}, {}, {}
        for n in TWIN_WEIGHTS:
            delta_w[n], new_m[n], new_v[n] = _adamw(weights[n], grad_w[n], given["m_" + n], given["v_" + n])
    return (loss, grad_x, *[grad_w[n] for n in TWIN_WEIGHTS], *[delta_w[n] for n in TWIN_WEIGHTS],
            *[new_m[n] for n in TWIN_WEIGHTS], *[new_v[n] for n in TWIN_WEIGHTS])
```

```python
import functools

import jax
import jax.numpy as jnp
from jax import lax
from jax.experimental import pallas as pl
from jax.experimental.pallas import tpu as pltpu

F32, BF16 = jnp.float32, jnp.bfloat16
MESH = pl.DeviceIdType.MESH
N_DEV = 8
N_LAYERS = 4
N_MOD = 6
HEAD_DIM = 128
POOL_WINDOWS = (2, 4, 8, 16)
EPS = 1e-6
ADAM_LR, ADAM_B1, ADAM_B2, ADAM_EPS, ADAM_WD, ADAM_STEP = 0.001, 0.9, 0.999, 1e-08, 0.01, 10

LANES = 128
HALO = 16
ROW_TILE = 512
CONV_TILE = 256
MM_TILE = 1024
FFN_TILE = 512
ATT_BLOCK = 256
ADAM_BLOCK = 256 * 1024
VMEM_LIMIT = 56 * 1024 * 1024

_NN = (((1,), (0,)), ((), ()))
_NT = (((1,), (1,)), ((), ()))
_TN = (((0,), (0,)), ((), ()))
_DIMS = {"nn": _NN, "nt": _NT, "tn": _TN}


def _cp(*sem):
    return pltpu.CompilerParams(dimension_semantics=sem if sem else None, vmem_limit_bytes=VMEM_LIMIT)


def _dot(a, b, dims=_NN):
    return lax.dot_general(a, b, dims, preferred_element_type=F32)


def _ld(ref, nlead):
    return ref[...] if nlead == 0 else ref[(0,) * nlead]


def _st(ref, nlead, val):
    if nlead == 0:
        ref[...] = val
    else:
        ref[(0,) * nlead] = val


def _row_tile(rows, cap):
    if rows <= cap:
        return rows
    t = cap - cap % 8
    while rows % t:
        t -= 8
    return t


def _vec_spec(d, nidx):
    zero = (0, 0)
    return pl.BlockSpec((1, d), {1: lambda i: zero, 2: lambda i, j: zero}[nidx])


def _my_index():
    return 4 * lax.axis_index("x") + 2 * lax.axis_index("y") + lax.axis_index("c")


def _peer(r):
    x, y, c = lax.axis_index("x"), lax.axis_index("y"), lax.axis_index("c")
    px = 1 - x if r & 4 else x
    py = 1 - y if r & 2 else y
    pc = 1 - c if r & 1 else c
    return (px, py, pc), 4 * px + 2 * py + pc


def _exchange(x, name, gather):
    out_shape = ((N_DEV,) + x.shape) if gather else x.shape

    def body(x_ref, o_ref, send_sems, recv_sems, local_sem):
        me = _my_index()

        def src(dst_idx):
            return x_ref if gather else x_ref.at[dst_idx]

        mine = pltpu.make_async_copy(src(me), o_ref.at[me], local_sem.at[0])
        mine.start()
        sends = []
        for r in range(1, N_DEV):
            peer, pidx = _peer(r)
            cp = pltpu.make_async_remote_copy(src_ref=src(pidx), dst_ref=o_ref.at[me], send_sem=send_sems.at[r - 1],
                                              recv_sem=recv_sems.at[r - 1], device_id=peer, device_id_type=MESH)
            cp.start()
            sends.append(cp)
        for r in range(1, N_DEV):
            peer, pidx = _peer(r)
            pltpu.make_async_remote_copy(src_ref=src(pidx), dst_ref=o_ref.at[pidx], send_sem=send_sems.at[r - 1],
                                         recv_sem=recv_sems.at[r - 1], device_id=peer, device_id_type=MESH).wait_recv()
        for cp in sends:
            cp.wait_send()
        mine.wait()

    return pl.pallas_call(
        body, name=name, out_shape=jax.ShapeDtypeStruct(out_shape, x.dtype),
        in_specs=[pl.BlockSpec(memory_space=pl.ANY)], out_specs=pl.BlockSpec(memory_space=pl.ANY),
        scratch_shapes=[pltpu.SemaphoreType.DMA((N_DEV - 1,)), pltpu.SemaphoreType.DMA((N_DEV - 1,)),
                        pltpu.SemaphoreType.DMA((1,))],
    )(x)


def all_gather(x, name):
    return _exchange(x, name, True)


def all_to_all(x, name):
    return _exchange(x, name, False)


def mm(a, b, *, mode, grid, a_blk, a_map, b_blk, b_map, o_shape, o_blk, o_map, o_dtype, name, into=None):
    nk = grid[2]
    na, nb, no = len(a_blk) - 2, len(b_blk) - 2, len(o_blk) - 2
    dims = _DIMS[mode]

    def body(*refs):
        if into is not None:
            a_ref, b_ref, _, o_ref, *scratch = refs
        else:
            a_ref, b_ref, o_ref, *scratch = refs
        p = _dot(_ld(a_ref, na), _ld(b_ref, nb), dims)
        if nk == 1:
            _st(o_ref, no, p.astype(o_dtype))
        else:
            acc = scratch[0]
            k = pl.program_id(2)

            @pl.when(k == 0)
            def _():
                acc[...] = p

            @pl.when(k > 0)
            def _():
                acc[...] += p

            @pl.when(k == nk - 1)
            def _():
                _st(o_ref, no, acc[...].astype(o_dtype))

    in_specs = [pl.BlockSpec(a_blk, a_map), pl.BlockSpec(b_blk, b_map)]
    args = [a, b]
    aliases = {}
    if into is not None:
        in_specs.append(pl.BlockSpec(memory_space=pl.ANY))
        args.append(into)
        aliases = {2: 0}
    return pl.pallas_call(
        body, name=name, grid=grid, out_shape=jax.ShapeDtypeStruct(o_shape, o_dtype),
        in_specs=in_specs, out_specs=pl.BlockSpec(o_blk, o_map),
        scratch_shapes=[pltpu.VMEM(tuple(o_blk[-2:]), F32)] if nk > 1 else [],
        input_output_aliases=aliases, compiler_params=_cp("parallel", "parallel", "arbitrary"),
    )(*args)


def proj_fwd(u, w8, o_dtype, name):
    S, D = u.shape
    nc = w8.shape[-1]
    T = min(MM_TILE, S)
    return mm(u, w8, mode="nn", grid=(S // T, N_DEV, 1), a_blk=(T, D), a_map=lambda i, j, k: (i, 0),
              b_blk=(1, 1, D, nc), b_map=lambda i, j, k: (j, 0, 0, 0), o_shape=(S, N_DEV * nc), o_blk=(T, nc),
              o_map=lambda i, j, k: (i, j), o_dtype=o_dtype, name=name)


def proj_bwd_x(dy, w8, name):
    S = dy.shape[0]
    D, nc = w8.shape[-2:]
    T = min(MM_TILE, S)
    return mm(dy, w8, mode="nt", grid=(S // T, 1, N_DEV), a_blk=(T, nc), a_map=lambda i, j, k: (i, k),
              b_blk=(1, 1, D, nc), b_map=lambda i, j, k: (k, 0, 0, 0), o_shape=(S, D), o_blk=(T, D),
              o_map=lambda i, j, k: (i, 0), o_dtype=F32, name=name)


def proj_bwd_w(u, dy, name):
    S, D = u.shape
    nc = dy.shape[1] // N_DEV
    T = min(MM_TILE, S)
    return mm(u, dy, mode="tn", grid=(N_DEV, 1, S // T), a_blk=(T, D), a_map=lambda j, _, k: (k, 0),
              b_blk=(T, nc), b_map=lambda j, _, k: (k, j), o_shape=(N_DEV, 1, D, nc), o_blk=(1, 1, D, nc),
              o_map=lambda j, _, k: (j, 0, 0, 0), o_dtype=BF16, name=name)


def dense_fwd(a, w, name):
    S, D = a.shape
    T = min(MM_TILE, S)
    return mm(a, w, mode="nn", grid=(S // T, 1, 1), a_blk=(T, D), a_map=lambda i, j, k: (i, 0), b_blk=(D, D),
              b_map=lambda i, j, k: (0, 0), o_shape=(S, D), o_blk=(T, D), o_map=lambda i, j, k: (i, 0),
              o_dtype=BF16, name=name)


def dense_bwd_x(dy, w, name):
    S, D = dy.shape
    T = min(MM_TILE, S)
    return mm(dy, w, mode="nt", grid=(S // T, 1, 1), a_blk=(T, D), a_map=lambda i, j, k: (i, 0), b_blk=(D, D),
              b_map=lambda i, j, k: (0, 0), o_shape=(S, D), o_blk=(T, D), o_map=lambda i, j, k: (i, 0),
              o_dtype=BF16, name=name)


def dense_bwd_w(a, dy, name):
    S, D = a.shape
    T = min(MM_TILE, S)
    half = D // 2
    return mm(a, dy, mode="tn", grid=(1, 2, S // T), a_blk=(T, D), a_map=lambda i, j, k: (k, 0), b_blk=(T, half),
              b_map=lambda i, j, k: (k, j), o_shape=(D, D), o_blk=(D, half), o_map=lambda i, j, k: (0, j),
              o_dtype=BF16, name=name)


def _mod_cols(nloc):
    return 256 if nloc % 256 == 0 else nloc


def mod_project(c_pad, w_mod, name):
    L, D, nloc = w_mod.shape
    R = c_pad.shape[0]
    tn = _mod_cols(nloc)

    def body(c_ref, w_ref, o_ref):
        c = c_ref[...]
        s = (c / (1.0 + jnp.exp(-c))).astype(BF16)
        o_ref[0] = _dot(s, w_ref[0].astype(BF16))

    return pl.pallas_call(
        body, name=name, grid=(L, nloc // tn), out_shape=jax.ShapeDtypeStruct((L, R, nloc), F32),
        in_specs=[pl.BlockSpec((R, D), lambda l, j: (0, 0)), pl.BlockSpec((1, D, tn), lambda l, j: (l, 0, j))],
        out_specs=pl.BlockSpec((1, R, tn), lambda l, j: (l, 0, j)), compiler_params=_cp("parallel", "parallel"),
    )(c_pad, w_mod)


def _adam(w, g, m, v):
    m = ADAM_B1 * m + (1.0 - ADAM_B1) * g
    v = ADAM_B2 * v + (1.0 - ADAM_B2) * (g * g)
    m_hat = m / (1.0 - ADAM_B1 ** ADAM_STEP)
    v_hat = v / (1.0 - ADAM_B2 ** ADAM_STEP)
    delta = -ADAM_LR * (m_hat / (jnp.sqrt(v_hat) + ADAM_EPS) + ADAM_WD * w)
    return delta, m, v


def mod_grad_adam(c_pad, dmod_pad, w, m, v, name):
    L, D, nloc = w.shape
    R = c_pad.shape[0]
    tn = _mod_cols(nloc)

    def body(c_ref, d_ref, w_ref, m_ref, v_ref, g_ref, dl_ref, nm_ref, nv_ref):
        c = c_ref[...]
        s = (c / (1.0 + jnp.exp(-c))).astype(BF16)
        g = _dot(s, d_ref[0].astype(BF16), _TN)
        delta, nm, nv = _adam(w_ref[0], g, m_ref[0], v_ref[0])
        g_ref[0], dl_ref[0], nm_ref[0], nv_ref[0] = g, delta, nm, nv

    wspec = pl.BlockSpec((1, D, tn), lambda l, j: (l, 0, j))
    out = jax.ShapeDtypeStruct(w.shape, F32)
    return pl.pallas_call(
        body, name=name, grid=(L, nloc // tn), out_shape=(out,) * 4,
        in_specs=[pl.BlockSpec((R, D), lambda l, j: (0, 0)), pl.BlockSpec((1, R, tn), lambda l, j: (l, 0, j)),
                  wspec, wspec, wspec],
        out_specs=(wspec,) * 4, compiler_params=_cp("parallel", "parallel"),
    )(c_pad, dmod_pad, w, m, v)


def resid_norm(h_prev, y_prev, gate_prev, gn, sc, sh, u_dtype, name):
    S, D = h_prev.shape
    T = min(ROW_TILE, S)
    has_res = y_prev is not None

    def body(*refs):
        if has_res:
            h_ref, y_ref, gate_ref, gn_ref, sc_ref, sh_ref, ho_ref, u_ref = refs
            h = h_ref[...] + gate_ref[...] * y_ref[...].astype(F32)
            ho_ref[...] = h
        else:
            h_ref, gn_ref, sc_ref, sh_ref, u_ref = refs
            h = h_ref[...]
        r = lax.rsqrt(jnp.mean(h * h, axis=-1, keepdims=True) + EPS)
        u = (h * r) * gn_ref[...] * (1.0 + sc_ref[...]) + sh_ref[...]
        u_ref[...] = u.astype(u_dtype)

    row = pl.BlockSpec((T, D), lambda i: (i, 0))
    vec = _vec_spec(D, 1)
    if has_res:
        h, u = pl.pallas_call(
            body, name=name, grid=(S // T,), out_shape=(jax.ShapeDtypeStruct((S, D), F32), jax.ShapeDtypeStruct((S, D), u_dtype)),
            in_specs=[row, row, vec, vec, vec, vec], out_specs=(row, row), compiler_params=_cp("parallel"),
        )(h_prev, y_prev, gate_prev, gn, sc, sh)
        return h, u
    u = pl.pallas_call(
        body, name=name, grid=(S // T,), out_shape=jax.ShapeDtypeStruct((S, D), u_dtype),
        in_specs=[row, vec, vec, vec], out_specs=row, compiler_params=_cp("parallel"),
    )(h_prev, gn, sc, sh)
    return h_prev, u


def gate_bwd(dh, y, gate, name):
    S, D = dh.shape
    T = min(ROW_TILE, S)

    def body(dh_ref, y_ref, gate_ref, dy_ref, dg_ref):
        i = pl.program_id(0)
        d = dh_ref[...]
        dy_ref[...] = (d * gate_ref[...]).astype(BF16)
        part = jnp.sum(d * y_ref[...].astype(F32), axis=0, keepdims=True)

        @pl.when(i == 0)
        def _():
            dg_ref[...] = part

        @pl.when(i > 0)
        def _():
            dg_ref[...] += part

    row = pl.BlockSpec((T, D), lambda i: (i, 0))
    vec = _vec_spec(D, 1)
    return pl.pallas_call(
        body, name=name, grid=(S // T,), out_shape=(jax.ShapeDtypeStruct((S, D), BF16), jax.ShapeDtypeStruct((1, D), F32)),
        in_specs=[row, row, vec], out_specs=(row, vec), compiler_params=_cp("arbitrary"),
    )(dh, y, gate)


def norm_bwd(du, h, dh_res, gn, sc, name):
    S, D = h.shape
    T = min(ROW_TILE, S)

    def body(du_ref, h_ref, dr_ref, gn_ref, sc_ref, dh_ref, dgn_ref, dsc_ref, dsh_ref):
        i = pl.program_id(0)
        d = du_ref[...].astype(F32)
        hh = h_ref[...]
        r = lax.rsqrt(jnp.mean(hh * hh, axis=-1, keepdims=True) + EPS)
        n = hh * r
        gn_v = gn_ref[...]
        dng = d * (1.0 + sc_ref[...])
        dn = dng * gn_v
        dh_ref[...] = dr_ref[...] + r * (dn - n * jnp.mean(dn * n, axis=-1, keepdims=True))
        parts = (jnp.sum(dng * n, axis=0, keepdims=True), jnp.sum(d * (n * gn_v), axis=0, keepdims=True),
                 jnp.sum(d, axis=0, keepdims=True))

        @pl.when(i == 0)
        def _():
            for ref, part in zip((dgn_ref, dsc_ref, dsh_ref), parts):
                ref[...] = part

        @pl.when(i > 0)
        def _():
            for ref, part in zip((dgn_ref, dsc_ref, dsh_ref), parts):
                ref[...] += part

    row = pl.BlockSpec((T, D), lambda i: (i, 0))
    vec = _vec_spec(D, 1)
    vshape = jax.ShapeDtypeStruct((1, D), F32)
    return pl.pallas_call(
        body, name=name, grid=(S // T,), out_shape=(jax.ShapeDtypeStruct((S, D), F32), vshape, vshape, vshape),
        in_specs=[row, row, row, vec, vec], out_specs=(row, vec, vec, vec), compiler_params=_cp("arbitrary"),
    )(du, h, dh_res, gn, sc)


def final_loss(h_prev, y_prev, gate_prev, fg, target, name):
    S, D = h_prev.shape
    T = min(ROW_TILE, S)

    def body(h_ref, y_ref, gate_ref, fg_ref, t_ref, dh_ref, dfg_ref, loss_ref):
        i = pl.program_id(0)
        h = h_ref[...] + gate_ref[...] * y_ref[...].astype(F32)
        r = lax.rsqrt(jnp.mean(h * h, axis=-1, keepdims=True) + EPS)
        n = h * r
        g = fg_ref[...]
        err = n * g - t_ref[...]
        dout = err * (1.0 / D)
        dn = dout * g
        dh_ref[...] = r * (dn - n * jnp.mean(dn * n, axis=-1, keepdims=True))
        dfg = jnp.sum(dout * n, axis=0, keepdims=True)
        part = 0.5 * jnp.sum(jnp.mean(err * err, axis=-1, keepdims=True), axis=0, keepdims=True)
        part = jnp.broadcast_to(part, (1, LANES))

        @pl.when(i == 0)
        def _():
            dfg_ref[...] = dfg
            loss_ref[...] = part

        @pl.when(i > 0)
        def _():
            dfg_ref[...] += dfg
            loss_ref[...] += part

    row = pl.BlockSpec((T, D), lambda i: (i, 0))
    vec = _vec_spec(D, 1)
    return pl.pallas_call(
        body, name=name, grid=(S // T,),
        out_shape=(jax.ShapeDtypeStruct((S, D), F32), jax.ShapeDtypeStruct((1, D), F32), jax.ShapeDtypeStruct((1, LANES), F32)),
        in_specs=[row, row, vec, vec, row], out_specs=(row, vec, pl.BlockSpec((1, LANES), lambda i: (0, 0))),
        compiler_params=_cp("arbitrary"),
    )(h_prev, y_prev, gate_prev, fg, target)


def _prev_halo(T):
    return lambda i: (jnp.maximum(i * (T // HALO) - 1, 0), 0)


def _next_halo(T, S):
    return lambda i: (jnp.minimum((i + 1) * (T // HALO), S // HALO - 1), 0)


def pool_fwd(u, w, ps, name):
    S, D = u.shape
    T = min(ROW_TILE, S)
    C = D // len(POOL_WINDOWS)

    def body(uc_ref, up_ref, w_ref, ps_ref, diff_ref, y_ref):
        i = pl.program_id(0)
        t = lax.broadcasted_iota(jnp.int32, (T, 1), 0) + i * T
        for g, win in enumerate(POOL_WINDOWS):
            cols = slice(g * C, (g + 1) * C)
            cur = uc_ref[:, cols]
            prev = jnp.where(i > 0, up_ref[:, cols], 0.0)
            s = jnp.concatenate([prev, cur], axis=0)
            k = 1
            while k < win:
                s = s + pltpu.roll(s, k, 0)
                k *= 2
            cnt = jnp.minimum(t + 1, win).astype(F32)
            diff = (s[HALO:, :] / cnt - cur).astype(BF16)
            diff_ref[:, cols] = diff
            y_ref[:, cols] = (_dot(diff, w_ref[g]) * ps_ref[:, cols]).astype(BF16)

    row = pl.BlockSpec((T, D), lambda i: (i, 0))
    out = jax.ShapeDtypeStruct((S, D), BF16)
    return pl.pallas_call(
        body, name=name, grid=(S // T,), out_shape=(out, out),
        in_specs=[row, pl.BlockSpec((HALO, D), _prev_halo(T)), pl.BlockSpec(w.shape, lambda i: (0, 0, 0)), _vec_spec(D, 1)],
        out_specs=(row, row), compiler_params=_cp("parallel"),
    )(u, u, w, ps)


def pool_bwd(dy, diff, w, ps, name):
    S, D = dy.shape
    T = min(ROW_TILE, S)
    G = len(POOL_WINDOWS)
    C = D // G

    def body(dy_ref, diff_ref, w_ref, ps_ref, dd_ref, dw_ref, dps_ref):
        i = pl.program_id(0)
        for g in range(G):
            cols = slice(g * C, (g + 1) * C)
            diff = diff_ref[:, cols]
            d = dy_ref[:, cols].astype(F32)
            ypre = _dot(diff, w_ref[g])
            dps = jnp.sum(d * ypre, axis=0, keepdims=True)
            dyp = (d * ps_ref[:, cols]).astype(BF16)
            dd_ref[:, cols] = _dot(dyp, w_ref[g], _NT)
            dw = _dot(diff, dyp, _TN)

            @pl.when(i == 0)
            def _():
                dw_ref[g] = dw
                dps_ref[:, cols] = dps

            @pl.when(i > 0)
            def _():
                dw_ref[g] += dw
                dps_ref[:, cols] += dps

    row = pl.BlockSpec((T, D), lambda i: (i, 0))
    wspec = pl.BlockSpec(w.shape, lambda i: (0, 0, 0))
    return pl.pallas_call(
        body, name=name, grid=(S // T,),
        out_shape=(jax.ShapeDtypeStruct((S, D), F32), jax.ShapeDtypeStruct(w.shape, F32), jax.ShapeDtypeStruct((1, D), F32)),
        in_specs=[row, row, wspec, _vec_spec(D, 1)], out_specs=(row, wspec, _vec_spec(D, 1)),
        compiler_params=_cp("arbitrary"),
    )(dy, diff, w, ps)


def pool_window_bwd(dd, name):
    S, D = dd.shape
    T = min(ROW_TILE, S)
    C = D // len(POOL_WINDOWS)
    n = T + HALO
    last = S // T - 1

    def body(dc_ref, dn_ref, du_ref):
        i = pl.program_id(0)
        t = lax.broadcasted_iota(jnp.int32, (n, 1), 0) + i * T
        for g, win in enumerate(POOL_WINDOWS):
            cols = slice(g * C, (g + 1) * C)
            cur = dc_ref[:, cols]
            nxt = jnp.where(i < last, dn_ref[:, cols], 0.0)
            cnt = jnp.minimum(t + 1, win).astype(F32)
            s = jnp.concatenate([cur, nxt], axis=0) / cnt
            k = 1
            while k < win:
                s = s + pltpu.roll(s, n - k, 0)
                k *= 2
            du_ref[:, cols] = s[:T, :] - cur

    row = pl.BlockSpec((T, D), lambda i: (i, 0))
    return pl.pallas_call(
        body, name=name, grid=(S // T,), out_shape=jax.ShapeDtypeStruct((S, D), F32),
        in_specs=[row, pl.BlockSpec((HALO, D), _next_halo(T, S))], out_specs=row, compiler_params=_cp("parallel"),
    )(dd, dd)


def _conv_chunk(D):
    return 512 if D % 512 == 0 else D


def conv_fwd(U, cw, name):
    S = U.shape[0]
    D = U.shape[1] // 3
    T = min(CONV_TILE, S)
    CH = _conv_chunk(D)

    def body(uc_ref, up_ref, cw_ref, q_ref):
        i = pl.program_id(0)
        for j in range(D // CH):
            cols = slice(j * CH, (j + 1) * CH)
            ccols = slice(D + j * CH, D + (j + 1) * CH)
            vcols = slice(2 * D + j * CH, 2 * D + (j + 1) * CH)
            zp = jnp.where(i > 0, up_ref[:, ccols] * up_ref[:, vcols], 0.0)
            z = jnp.concatenate([zp, uc_ref[:, ccols] * uc_ref[:, vcols]], axis=0)
            zc = cw_ref[2, :, cols] * z + cw_ref[1, :, cols] * pltpu.roll(z, 1, 0) + cw_ref[0, :, cols] * pltpu.roll(z, 2, 0)
            q_ref[:, cols] = (uc_ref[:, cols] * zc[HALO:, :]).astype(BF16)

    return pl.pallas_call(
        body, name=name, grid=(S // T,), out_shape=jax.ShapeDtypeStruct((S, D), BF16),
        in_specs=[pl.BlockSpec((T, 3 * D), lambda i: (i, 0)), pl.BlockSpec((HALO, 3 * D), _prev_halo(T)),
                  pl.BlockSpec((3, 1, D), lambda i: (0, 0, 0))],
        out_specs=pl.BlockSpec((T, D), lambda i: (i, 0)), compiler_params=_cp("parallel"),
    )(U, U, cw)


def conv_bwd(dq, U, cw, name):
    S = U.shape[0]
    D = U.shape[1] // 3
    T = min(CONV_TILE, S)
    CH = _conv_chunk(D)
    n = T + HALO
    last = S // T - 1

    def body(dq_ref, dqn_ref, uc_ref, up_ref, un_ref, cw_ref, du_ref, dcw_ref):
        i = pl.program_id(0)
        for j in range(D // CH):
            cols = slice(j * CH, (j + 1) * CH)
            ccols = slice(D + j * CH, D + (j + 1) * CH)
            vcols = slice(2 * D + j * CH, 2 * D + (j + 1) * CH)
            w0, w1, w2 = cw_ref[0, :, cols], cw_ref[1, :, cols], cw_ref[2, :, cols]
            b, c, v = uc_ref[:, cols], uc_ref[:, ccols], uc_ref[:, vcols]
            dqc = dq_ref[:, cols].astype(F32)
            zp = jnp.where(i > 0, up_ref[:, ccols] * up_ref[:, vcols], 0.0)
            z = jnp.concatenate([zp, c * v], axis=0)
            z1 = pltpu.roll(z, 1, 0)[HALO:, :]
            z2 = pltpu.roll(z, 2, 0)[HALO:, :]
            z0 = z[HALO:, :]
            zc = w2 * z0 + w1 * z1 + w0 * z2
            dzc = dqc * b
            dzn = jnp.where(i < last, dqn_ref[:, cols].astype(F32) * un_ref[:, cols], 0.0)
            e = jnp.concatenate([dzc, dzn], axis=0)
            dz = (w2 * e + w1 * pltpu.roll(e, n - 1, 0) + w0 * pltpu.roll(e, n - 2, 0))[:T, :]
            du_ref[:, cols] = (dqc * zc).astype(BF16)
            du_ref[:, ccols] = (dz * v).astype(BF16)
            du_ref[:, vcols] = (dz * c).astype(BF16)
            parts = [jnp.sum(dzc * zz, axis=0, keepdims=True) for zz in (z2, z1, z0)]

            @pl.when(i == 0)
            def _():
                for k in range(3):
                    dcw_ref[k, :, cols] = parts[k]

            @pl.when(i > 0)
            def _():
                for k in range(3):
                    dcw_ref[k, :, cols] += parts[k]

    return pl.pallas_call(
        body, name=name, grid=(S // T,),
        out_shape=(jax.ShapeDtypeStruct((S, 3 * D), BF16), jax.ShapeDtypeStruct((3, 1, D), F32)),
        in_specs=[pl.BlockSpec((T, D), lambda i: (i, 0)), pl.BlockSpec((HALO, D), _next_halo(T, S)),
                  pl.BlockSpec((T, 3 * D), lambda i: (i, 0)), pl.BlockSpec((HALO, 3 * D), _prev_halo(T)),
                  pl.BlockSpec((HALO, 3 * D), _next_halo(T, S)), pl.BlockSpec((3, 1, D), lambda i: (0, 0, 0))],
        out_specs=(pl.BlockSpec((T, 3 * D), lambda i: (i, 0)), pl.BlockSpec((3, 1, D), lambda i: (0, 0, 0))),
        compiler_params=_cp("arbitrary"),
    )(dq, dq, U, U, U, cw)


def _sb_logits(qb, kblk, q0, k0, B):
    row = lax.broadcasted_iota(jnp.int32, (B, B), 0)
    col = lax.broadcasted_iota(jnp.int32, (B, B), 1)
    z = _dot(qb, kblk, _NT) * (HEAD_DIM ** -0.5)
    causal = (col + k0) < (row + q0)
    lb = jnp.minimum(z, 0.0) - jnp.log(1.0 + jnp.exp(-jnp.abs(z)))
    l1 = jnp.where(causal, lb - z, 0.0)
    return lb, l1, causal


def _key_sums(l1, tri):
    hi = l1.astype(BF16)
    lo = (l1 - hi.astype(F32)).astype(BF16)
    return _dot(hi, tri) + _dot(lo, tri)


def _tri(B, cmp):
    row = lax.broadcasted_iota(jnp.int32, (B, B), 0)
    col = lax.broadcasted_iota(jnp.int32, (B, B), 1)
    return cmp(row, col).astype(BF16)


def attn_fwd(qkv, name):
    S = qkv.shape[0]
    D = qkv.shape[1] // 3
    H = D // HEAD_DIM
    B = min(ATT_BLOCK, S)

    def body(q_ref, k_ref, v_ref, o_ref, lt_ref):
        later = _tri(B, lambda j, s: j > s)

        def q_loop(qi, _):
            q0 = pl.multiple_of(qi * B, B)
            qb = q_ref[pl.ds(q0, B), :]

            def k_loop(kk, state):
                carry, oacc = state
                k0 = pl.multiple_of((qi - kk) * B, B)
                lb, l1, causal = _sb_logits(qb, k_ref[pl.ds(k0, B), :], q0, k0, B)
                suffix = _key_sums(l1, later) + carry
                a = jnp.where(causal, jnp.exp(lb + suffix), 0.0)
                oacc = oacc + _dot(a.astype(BF16), v_ref[pl.ds(k0, B), :])
                return carry + jnp.sum(l1, axis=1, keepdims=True), oacc

            total, oacc = lax.fori_loop(0, qi + 1, k_loop, (jnp.zeros((B, 1), F32), jnp.zeros((B, HEAD_DIM), F32)))
            o_ref[pl.ds(q0, B), :] = oacc.astype(BF16)
            lt_ref[pl.ds(q0, B), :] = jnp.broadcast_to(total, (B, HEAD_DIM))
            return 0

        lax.fori_loop(0, S // B, q_loop, 0)

    def head(off):
        return pl.BlockSpec((S, HEAD_DIM), lambda h: (0, off + h))

    return pl.pallas_call(
        body, name=name, grid=(H,), out_shape=(jax.ShapeDtypeStruct((S, D), BF16), jax.ShapeDtypeStruct((S, D), F32)),
        in_specs=[head(0), head(H), head(2 * H)], out_specs=(head(0), head(0)), compiler_params=_cp("parallel"),
    )(qkv, qkv, qkv)


def attn_bwd(qkv, lt, do, name):
    S = qkv.shape[0]
    D = qkv.shape[1] // 3
    H = D // HEAD_DIM
    B = min(ATT_BLOCK, S)
    scale = HEAD_DIM ** -0.5

    def body(q_ref, k_ref, v_ref, lt_ref, do_ref, dq_ref, dk_ref, dv_ref, dk_acc, dv_acc):
        upto = _tri(B, lambda j, s: j <= s)
        before = _tri(B, lambda j, s: j < s)
        dk_acc[...] = jnp.zeros_like(dk_acc)
        dv_acc[...] = jnp.zeros_like(dv_acc)

        def q_loop(qi, _):
            q0 = pl.multiple_of(qi * B, B)
            qb = q_ref[pl.ds(q0, B), :]
            dob = do_ref[pl.ds(q0, B), :]
            total = lt_ref[pl.ds(q0, B), :][:, :1]

            def k_loop(kb, state):
                lsum, gsum, dqacc = state
                k0 = pl.multiple_of(kb * B, B)
                kblk = k_ref[pl.ds(k0, B), :]
                lb, l1, causal = _sb_logits(qb, kblk, q0, k0, B)
                suffix = total - lsum - _key_sums(l1, upto)
                a = jnp.where(causal, jnp.exp(lb + suffix), 0.0)
                g = a * _dot(dob, v_ref[pl.ds(k0, B), :], _NT)
                gpre = gsum + _dot(g.astype(BF16), before)
                beta = jnp.exp(lb)
                dz = jnp.where(causal, (g * (1.0 - beta) - gpre * beta) * scale, 0.0).astype(BF16)
                dqacc = dqacc + _dot(dz, kblk)
                dk_acc[pl.ds(k0, B), :] += _dot(dz, qb, _TN)
                dv_acc[pl.ds(k0, B), :] += _dot(a.astype(BF16), dob, _TN)
                return (lsum + jnp.sum(l1, axis=1, keepdims=True), gsum + jnp.sum(g, axis=1, keepdims=True), dqacc)

            zero = jnp.zeros((B, 1), F32)
            _, _, dqacc = lax.fori_loop(0, qi + 1, k_loop, (zero, zero, jnp.zeros((B, HEAD_DIM), F32)))
            dq_ref[pl.ds(q0, B), :] = dqacc.astype(BF16)
            return 0

        lax.fori_loop(0, S // B, q_loop, 0)
        dk_ref[...] = dk_acc[...].astype(BF16)
        dv_ref[...] = dv_acc[...].astype(BF16)

    def head(off):
        return pl.BlockSpec((S, HEAD_DIM), lambda h: (0, off + h))

    out = jax.ShapeDtypeStruct((S, D), BF16)
    return pl.pallas_call(
        body, name=name, grid=(H,), out_shape=(out, out, out),
        in_specs=[head(0), head(H), head(2 * H), head(0), head(0)], out_specs=(head(0), head(0), head(0)),
        scratch_shapes=[pltpu.VMEM((S, HEAD_DIM), F32), pltpu.VMEM((S, HEAD_DIM), F32)], compiler_params=_cp("parallel"),
    )(qkv, qkv, qkv, lt, do)


def ffn_fwd(u, wgu8, wd8, l, name):
    S, D = u.shape
    n = wgu8.shape[-1]
    T = min(FFN_TILE, S)

    def body(u_ref, wgu_ref, wd_ref, gp_ref, y_ref, acc):
        j = pl.program_id(1)
        uu = u_ref[...]
        g = _dot(uu, wgu_ref[0, 0, 0])
        p = _dot(uu, wgu_ref[0, 1, 0])
        gp_ref[0, 0] = g.astype(BF16)
        gp_ref[1, 0] = p.astype(BF16)
        a = (g / (1.0 + jnp.exp(-g)) * p).astype(BF16)
        part = _dot(a, wd_ref[0, 0])

        @pl.when(j == 0)
        def _():
            acc[...] = part

        @pl.when(j > 0)
        def _():
            acc[...] += part

        @pl.when(j == N_DEV - 1)
        def _():
            y_ref[...] = acc[...].astype(BF16)

    return pl.pallas_call(
        body, name=name, grid=(S // T, N_DEV),
        out_shape=(jax.ShapeDtypeStruct((2, N_DEV, S, n), BF16), jax.ShapeDtypeStruct((S, D), BF16)),
        in_specs=[pl.BlockSpec((T, D), lambda i, j: (i, 0)), pl.BlockSpec((1, 2, 1, D, n), lambda i, j: (j, 0, l, 0, 0)),
                  pl.BlockSpec((1, 1, n, D), lambda i, j: (j, l, 0, 0))],
        out_specs=(pl.BlockSpec((2, 1, T, n), lambda i, j: (0, j, i, 0)), pl.BlockSpec((T, D), lambda i, j: (i, 0))),
        scratch_shapes=[pltpu.VMEM((T, D), F32)], compiler_params=_cp("parallel", "arbitrary"),
    )(u, wgu8, wd8)


def ffn_bwd_act(dy, gp, wd8, l, name):
    S, D = dy.shape
    n = gp.shape[-1]
    T = min(FFN_TILE, S)

    def body(dy_ref, gp_ref, wd_ref, a_ref, dgp_ref):
        da = _dot(dy_ref[...], wd_ref[0, 0], _NT)
        g = gp_ref[0, 0].astype(F32)
        p = gp_ref[1, 0].astype(F32)
        s = 1.0 / (1.0 + jnp.exp(-g))
        sl = g * s
        a_ref[0] = (sl * p).astype(BF16)
        dgp_ref[0, 0] = (da * p * (s * (1.0 + g * (1.0 - s)))).astype(BF16)
        dgp_ref[1, 0] = (da * sl).astype(BF16)

    gp_spec = pl.BlockSpec((2, 1, T, n), lambda i, j: (0, j, i, 0))
    return pl.pallas_call(
        body, name=name, grid=(S // T, N_DEV),
        out_shape=(jax.ShapeDtypeStruct((N_DEV, S, n), BF16), jax.ShapeDtypeStruct((2, N_DEV, S, n), BF16)),
        in_specs=[pl.BlockSpec((T, D), lambda i, j: (i, 0)), gp_spec, pl.BlockSpec((1, 1, n, D), lambda i, j: (j, l, 0, 0))],
        out_specs=(pl.BlockSpec((1, T, n), lambda i, j: (j, i, 0)), gp_spec), compiler_params=_cp("parallel", "parallel"),
    )(dy, gp, wd8)


def ffn_bwd_x(dgp, wgu8, l, name):
    S, n = dgp.shape[-2:]
    D = wgu8.shape[-2]
    T = min(MM_TILE, S)
    return mm(dgp, wgu8, mode="nt", grid=(S // T, 1, 2 * N_DEV), a_blk=(1, 1, T, n),
              a_map=lambda i, j, k: (k // N_DEV, k % N_DEV, i, 0), b_blk=(1, 1, 1, D, n),
              b_map=lambda i, j, k: (k % N_DEV, k // N_DEV, l, 0, 0), o_shape=(S, D), o_blk=(T, D),
              o_map=lambda i, j, k: (i, 0), o_dtype=F32, name=name)


def ffn_bwd_wgu(u, dgp, into, l, name):
    S, D = u.shape
    n = dgp.shape[-1]
    T = min(MM_TILE, S)
    return mm(u, dgp, mode="tn", grid=(N_DEV, 2, S // T), a_blk=(T, D), a_map=lambda j, t, k: (k, 0),
              b_blk=(1, 1, T, n), b_map=lambda j, t, k: (t, j, k, 0), o_shape=into.shape, o_blk=(1, 1, 1, D, n),
              o_map=lambda j, t, k: (j, t, l, 0, 0), o_dtype=BF16, name=name, into=into)


def ffn_bwd_wd(a8, dy, into, l, name):
    n = a8.shape[-1]
    S, D = dy.shape
    T = min(MM_TILE, S)
    return mm(a8, dy, mode="tn", grid=(N_DEV, 1, S // T), a_blk=(1, T, n), a_map=lambda j, _, k: (j, k, 0),
              b_blk=(T, D), b_map=lambda j, _, k: (k, 0), o_shape=into.shape, o_blk=(1, 1, n, D),
              o_map=lambda j, _, k: (j, l, 0, 0), o_dtype=BF16, name=name, into=into)


def adam_update(recv, a_idx, w, m, v, name):
    L, R, C = w.shape
    TR = _row_tile(R, max(8, ADAM_BLOCK // C))
    nsrc = recv.shape[0]

    def body(r_ref, w_ref, m_ref, v_ref, g_ref, d_ref, nm_ref, nv_ref):
        g = r_ref[0, 0, 0].astype(F32)
        for k in range(1, nsrc):
            g = g + r_ref[k, 0, 0].astype(F32)
        delta, nm, nv = _adam(w_ref[0], g, m_ref[0], v_ref[0])
        g_ref[0], d_ref[0], nm_ref[0], nv_ref[0] = g, delta, nm, nv

    wspec = pl.BlockSpec((1, TR, C), lambda l, i: (l, i, 0))
    out = jax.ShapeDtypeStruct(w.shape, F32)
    return pl.pallas_call(
        body, name=name, grid=(L, R // TR), out_shape=(out,) * 4,
        in_specs=[pl.BlockSpec((nsrc, 1, 1, TR, C), lambda l, i: (0, a_idx, l, i, 0)), wspec, wspec, wspec],
        out_specs=(wspec,) * 4, compiler_params=_cp("parallel", "parallel"),
    )(recv, w, m, v)


def sum_devices(x, name):
    _, R, C = x.shape

    def body(x_ref, o_ref):
        s = x_ref[0]
        for k in range(1, N_DEV):
            s = s + x_ref[k]
        o_ref[...] = s

    return pl.pallas_call(
        body, name=name, out_shape=jax.ShapeDtypeStruct((R, C), F32),
        in_specs=[pl.BlockSpec(memory_space=pltpu.VMEM)], out_specs=pl.BlockSpec(memory_space=pltpu.VMEM),
    )(x)


def _pack(parts):
    flat = jnp.concatenate([p.reshape(-1) for p in parts])
    pad = (-flat.shape[0]) % (8 * LANES)
    return jnp.pad(flat, (0, pad)).reshape(-1, LANES)


def _unpack(packed, shapes, lead=()):
    flat = packed.reshape(lead + (-1,))
    out, off = [], 0
    for s in shapes:
        size = 1
        for d in s:
            size *= d
        out.append(flat[..., off:off + size].reshape(lead + tuple(s)))
        off += size
    return out


def kernel(x, c, norm_mix_g, norm_ffn_g, w_mod, b_mod, pool_w, pool_scale, conv_w_in, conv_w, conv_w_out, sb_w_qkv, sb_w_o, ffn_w_gate, ffn_w_up, ffn_w_down, final_g, loss_target, m_norm_mix_g, m_norm_ffn_g, m_w_mod, m_b_mod, m_pool_w, m_pool_scale, m_conv_w_in, m_conv_w, m_conv_w_out, m_sb_w_qkv, m_sb_w_o, m_ffn_w_gate, m_ffn_w_up, m_ffn_w_down, m_final_g, v_norm_mix_g, v_norm_ffn_g, v_w_mod, v_b_mod, v_pool_w, v_pool_scale, v_conv_w_in, v_conv_w, v_conv_w_out, v_sb_w_qkv, v_sb_w_o, v_ffn_w_gate, v_ffn_w_up, v_ffn_w_down, v_final_g):
    S, D = x.shape[1:]
    L = N_LAYERS
    G = len(POOL_WINDOWS)
    C = D // G
    dloc = D // N_DEV
    nmod = w_mod.shape[-1]
    me = _my_index()

    wgu8 = all_gather(jnp.stack([ffn_w_gate, ffn_w_up]).astype(BF16), "ag_ffn_gate_up")
    wd8 = all_gather(ffn_w_down.astype(BF16), "ag_ffn_down")
    wcin8 = all_gather(conv_w_in.astype(BF16), "ag_conv_in")
    wcout = all_gather(conv_w_out.astype(BF16), "ag_conv_out").reshape(D, D)
    wqkv8 = all_gather(sb_w_qkv.astype(BF16), "ag_sb_qkv")
    wo = all_gather(sb_w_o.astype(BF16), "ag_sb_o").reshape(D, D)
    pw = all_gather(pool_w.astype(BF16), "ag_pool_w")
    pw = pw.transpose(1, 2, 0, 3, 4).reshape(pool_w.shape[0], G, C, C)
    small_shapes = [(1, D), (3, dloc), (pool_scale.shape[0], dloc)]
    small = all_gather(_pack([c, conv_w, pool_scale]), "ag_small_in")
    c_all, cw_all, ps_all = _unpack(small, small_shapes, (N_DEV,))
    c_pad = jnp.pad(c_all.reshape(N_DEV, D), ((0, N_DEV), (0, 0)))
    cw = cw_all.transpose(1, 0, 2).reshape(3, 1, D)
    ps = ps_all.transpose(1, 0, 2).reshape(-1, D)

    mod_part = mod_project(c_pad, w_mod, "mod_project")
    mod_all = all_gather(mod_part, "ag_mod")
    mod = lax.dynamic_index_in_dim(mod_all, me, axis=2, keepdims=False)
    mod = mod.transpose(1, 0, 2).reshape(L, N_MOD * D) + b_mod

    def mod_vec(i, k):
        return mod[i, k * D:(k + 1) * D].reshape(1, D)

    h = x[0]
    y_prev = gate_prev = None
    saved = []
    for i in range(L):
        sh_m, sc_m, g_m, sh_f, sc_f, g_f = (mod_vec(i, k) for k in range(N_MOD))
        kind, j = i % 3, i // 3
        gn_m, gn_f = norm_mix_g[i].reshape(1, D), norm_ffn_g[i].reshape(1, D)
        h_mix, u = resid_norm(h, y_prev, gate_prev, gn_m, sc_m, sh_m, F32 if kind == 0 else BF16, f"norm_mix_{i}")
        if kind == 0:
            diff, y = pool_fwd(u, pw[j], ps[j].reshape(1, D), f"pool_fwd_{i}")
            mix = (diff,)
        elif kind == 1:
            U = proj_fwd(u, wcin8, F32, f"conv_in_{i}")
            q = conv_fwd(U, cw, f"conv_fwd_{i}")
            y = dense_fwd(q, wcout, f"conv_out_{i}")
            mix = (u, U, q)
        else:
            qkv = proj_fwd(u, wqkv8, BF16, f"sb_qkv_{i}")
            o, lt = attn_fwd(qkv, f"attn_fwd_{i}")
            y = dense_fwd(o, wo, f"sb_out_{i}")
            mix = (u, qkv, o, lt)
        h_ffn, u_ffn = resid_norm(h_mix, y, g_m, gn_f, sc_f, sh_f, BF16, f"norm_ffn_{i}")
        gp, y_ffn = ffn_fwd(u_ffn, wgu8, wd8, i, f"ffn_fwd_{i}")
        saved.append((h_mix, mix, y, h_ffn, u_ffn, gp, y_ffn))
        h, y_prev, gate_prev = h_ffn, y_ffn, g_f

    dh, dfg, loss_part = final_loss(h, y_prev, gate_prev, final_g.reshape(1, D), loss_target[0], "final_loss")
    loss = lax.psum(loss_part[0, 0], ("x", "y", "c"))

    dwgu8 = jnp.zeros(wgu8.shape, BF16)
    dwd8 = jnp.zeros(wd8.shape, BF16)
    dpw, dps, dmod, dgn_mix, dgn_ffn = {}, {}, [None] * L, [None] * L, [None] * L
    for i in reversed(range(L)):
        sh_m, sc_m, g_m, sh_f, sc_f, g_f = (mod_vec(i, k) for k in range(N_MOD))
        kind, j = i % 3, i // 3
        gn_m, gn_f = norm_mix_g[i].reshape(1, D), norm_ffn_g[i].reshape(1, D)
        h_mix, mix, y, h_ffn, u_ffn, gp, y_ffn = saved[i]

        dy, dg_f = gate_bwd(dh, y_ffn, g_f, f"gate_bwd_ffn_{i}")
        a8, dgp = ffn_bwd_act(dy, gp, wd8, i, f"ffn_bwd_act_{i}")
        du = ffn_bwd_x(dgp, wgu8, i, f"ffn_bwd_x_{i}")
        dwgu8 = ffn_bwd_wgu(u_ffn, dgp, dwgu8, i, f"ffn_bwd_wgu_{i}")
        dwd8 = ffn_bwd_wd(a8, dy, dwd8, i, f"ffn_bwd_wd_{i}")
        dh, dgn_ffn[i], dsc_f, dsh_f = norm_bwd(du, h_ffn, dh, gn_f, sc_f, f"norm_bwd_ffn_{i}")

        dy, dg_m = gate_bwd(dh, y, g_m, f"gate_bwd_mix_{i}")
        if kind == 0:
            (diff,) = mix
            dd, dpw[j], dps[j] = pool_bwd(dy, diff, pw[j], ps[j].reshape(1, D), f"pool_bwd_{i}")
            du = pool_window_bwd(dd, f"pool_window_bwd_{i}")
        elif kind == 1:
            u, U, q = mix
            dq = dense_bwd_x(dy, wcout, f"conv_out_bwd_x_{i}")
            dwcout = dense_bwd_w(q, dy, f"conv_out_bwd_w_{i}")
            dU, dcw = conv_bwd(dq, U, cw, f"conv_bwd_{i}")
            du = proj_bwd_x(dU, wcin8, f"conv_in_bwd_x_{i}")
            dwcin8 = proj_bwd_w(u, dU, f"conv_in_bwd_w_{i}")
        else:
            u, qkv, o, lt = mix
            do = dense_bwd_x(dy, wo, f"sb_out_bwd_x_{i}")
            dwo = dense_bwd_w(o, dy, f"sb_out_bwd_w_{i}")
            dq_, dk_, dv_ = attn_bwd(qkv, lt, do, f"attn_bwd_{i}")
            dqkv = jnp.concatenate([dq_, dk_, dv_], axis=1)
            du = proj_bwd_x(dqkv, wqkv8, f"sb_qkv_bwd_x_{i}")
            dwqkv8 = proj_bwd_w(u, dqkv, f"sb_qkv_bwd_w_{i}")
        dh, dgn_mix[i], dsc_m, dsh_m = norm_bwd(du, h_mix, dh, gn_m, sc_m, f"norm_bwd_mix_{i}")
        dmod[i] = jnp.concatenate([dsh_m, dsc_m, dg_m, dsh_f, dsc_f, dg_f], axis=1)
    grad_x = dh[None]

    def reduce_update(g8, w, m_, v_, name):
        recv = all_to_all(g8, "a2a_" + name)
        w3 = w.reshape((-1,) + w.shape[-2:])
        recv = recv.reshape((N_DEV, 1) + w3.shape)
        outs = adam_update(recv, 0, w3, m_.reshape(w3.shape), v_.reshape(w3.shape), "adam_" + name)
        return [o_.reshape(w.shape) for o_ in outs]

    recv_gu = all_to_all(dwgu8, "a2a_ffn_gate_up")
    up_gate = adam_update(recv_gu, 0, ffn_w_gate, m_ffn_w_gate, v_ffn_w_gate, "adam_ffn_gate")
    up_up = adam_update(recv_gu, 1, ffn_w_up, m_ffn_w_up, v_ffn_w_up, "adam_ffn_up")
    up_down = reduce_update(dwd8, ffn_w_down, m_ffn_w_down, v_ffn_w_down, "ffn_down")
    up_cin = reduce_update(dwcin8, conv_w_in, m_conv_w_in, v_conv_w_in, "conv_in")
    up_cout = reduce_update(dwcout.reshape(N_DEV, 1, dloc, D), conv_w_out, m_conv_w_out, v_conv_w_out, "conv_out")
    up_qkv = reduce_update(dwqkv8, sb_w_qkv, m_sb_w_qkv, v_sb_w_qkv, "sb_qkv")
    up_o = reduce_update(dwo.reshape(N_DEV, 1, dloc, D), sb_w_o, m_sb_w_o, v_sb_w_o, "sb_o")
    dpw_all = jnp.stack([dpw[j] for j in range(pool_w.shape[0])])
    dpw8 = dpw_all.reshape(-1, G, N_DEV, C // N_DEV, C).transpose(2, 0, 1, 3, 4).astype(BF16)
    up_pool = reduce_update(dpw8, pool_w, m_pool_w, v_pool_w, "pool_w")

    dps_all = jnp.concatenate([dps[j] for j in range(pool_scale.shape[0])], axis=0)
    dmod_loc = jnp.concatenate(dmod, axis=0)
    part_shapes = [(L, D), (L, D), (1, D), (L, N_MOD * D), dps_all.shape, (3, 1, D)]
    parts8 = all_gather(_pack([jnp.concatenate(dgn_mix, axis=0), jnp.concatenate(dgn_ffn, axis=0), dfg, dmod_loc,
                               dps_all, dcw]), "ag_small_out")
    g_mix, g_ffn, g_fin, g_bmod, g_ps, g_cw = _unpack(sum_devices(parts8, "sum_small"), part_shapes)
    g_ps = lax.dynamic_slice_in_dim(g_ps, me * dloc, dloc, axis=1)
    g_cw = lax.dynamic_slice_in_dim(g_cw.reshape(3, D), me * dloc, dloc, axis=1)
    dmod_all = _unpack(parts8, part_shapes, (N_DEV,))[3]
    dmod_cols = lax.dynamic_slice_in_dim(dmod_all, me * nmod, nmod, axis=2)
    dmod_pad = jnp.pad(dmod_cols.transpose(1, 0, 2), ((0, 0), (0, N_DEV), (0, 0)))
    up_wmod = mod_grad_adam(c_pad, dmod_pad, w_mod, m_w_mod, v_w_mod, "mod_grad_adam")

    small_w = [norm_mix_g, norm_ffn_g, final_g, b_mod, pool_scale, conv_w]
    small_m = [m_norm_mix_g, m_norm_ffn_g, m_final_g, m_b_mod, m_pool_scale, m_conv_w]
    small_v = [v_norm_mix_g, v_norm_ffn_g, v_final_g, v_b_mod, v_pool_scale, v_conv_w]
    small_g = [g_mix, g_ffn, g_fin.reshape(final_g.shape), g_bmod, g_ps, g_cw.reshape(conv_w.shape)]
    packed = [_pack(t)[None] for t in (small_w, small_m, small_v)]
    up_small = adam_update(_pack(small_g)[None, None, None], 0, *packed, "adam_small")
    sshapes = [t.shape for t in small_w]
    up_small = [_unpack(t[0], sshapes) for t in up_small]
    (s_mix, s_ffn, s_fin, s_bmod, s_ps, s_cw) = zip(*up_small)

    per_weight = [s_mix, s_ffn, up_wmod, s_bmod, up_pool, s_ps, up_cin, s_cw, up_cout, up_qkv, up_o,
                  up_gate, up_up, up_down, s_fin]
    outs = [loss, grad_x]
    for kind_idx in range(4):
        outs.extend(t[kind_idx] for t in per_weight)
    return tuple(outs)
```

```python
import functools

import jax
import jax.numpy as jnp
from jax import lax
from jax.experimental import pallas as pl
from jax.experimental.pallas import tpu as pltpu

F32, BF16 = jnp.float32, jnp.bfloat16
MESH = pl.DeviceIdType.MESH
N_DEV = 8
N_LAYERS = 4
N_MOD = 6
HEAD_DIM = 128
POOL_WINDOWS = (2, 4, 8, 16)
EPS = 1e-6
ADAM_LR, ADAM_B1, ADAM_B2, ADAM_EPS, ADAM_WD, ADAM_STEP = 0.001, 0.9, 0.999, 1e-08, 0.01, 10

LANES = 128
HALO = 16
ROW_TILE = 512
CONV_TILE = 256
MM_TILE = 1024
FFN_TILE = 512
ATT_K = 256
ATT_Q = 1024
ATT_Q_BWD = 512
ADAM_BLOCK = 256 * 1024
VMEM_LIMIT = 56 * 1024 * 1024

_NN = (((1,), (0,)), ((), ()))
_NT = (((1,), (1,)), ((), ()))
_TN = (((0,), (0,)), ((), ()))
_DIMS = {"nn": _NN, "nt": _NT, "tn": _TN}


def _cp(*sem):
    return pltpu.CompilerParams(dimension_semantics=sem if sem else None, vmem_limit_bytes=VMEM_LIMIT)


def _dot(a, b, dims=_NN):
    return lax.dot_general(a, b, dims, preferred_element_type=F32)


def _ld(ref, nlead):
    return ref[...] if nlead == 0 else ref[(0,) * nlead]


def _st(ref, nlead, val):
    if nlead == 0:
        ref[...] = val
    else:
        ref[(0,) * nlead] = val


def _row_tile(rows, cap, mult=8):
    if rows <= cap:
        return rows
    t = cap - cap % mult
    while rows % t:
        t -= mult
    return t


def _vec_spec(d, nidx):
    zero = (0, 0)
    return pl.BlockSpec((1, d), {1: lambda i: zero, 2: lambda i, j: zero}[nidx])


def _my_index():
    return 4 * lax.axis_index("x") + 2 * lax.axis_index("y") + lax.axis_index("c")


_HBM = pl.BlockSpec(memory_space=pl.ANY)
N_CHIPS = N_DEV // 2


def _remote(src, dst, send_sem, recv_sem, to):
    return pltpu.make_async_remote_copy(src_ref=src, dst_ref=dst, send_sem=send_sem, recv_sem=recv_sem,
                                        device_id=to, device_id_type=MESH)


def _dma_sems(n):
    return [pltpu.SemaphoreType.DMA((n,)), pltpu.SemaphoreType.DMA((n,)), pltpu.SemaphoreType.DMA((1,))]


def all_gather(x, name):
    def body(x_ref, o_ref, send_sems, recv_sems, local_sem):
        x, y, c = lax.axis_index("x"), lax.axis_index("y"), lax.axis_index("c")
        me, sibling = (x, y, c), (x, y, 1 - c)
        chips = [(1 - x, y), (x, 1 - y), (1 - x, 1 - y)]

        def slot(px, py, pc):
            return o_ref.at[4 * px + 2 * py + pc]

        def copy(k, block, to, src=None):
            return _remote(slot(*block) if src is None else src, slot(*block), send_sems.at[k], recv_sems.at[k], to)

        mine = pltpu.make_async_copy(x_ref, slot(*me), local_sem.at[0])
        mine.start()
        first = [copy(0, me, sibling, src=x_ref)]
        first += [copy(1 + j, me, (*chip, c), src=x_ref) for j, chip in enumerate(chips)]
        for cp in first:
            cp.start()
        passed = [copy(4 + j, (*chip, c), sibling) for j, chip in enumerate(chips)]
        for j, chip in enumerate(chips):
            copy(1 + j, (*chip, c), me).wait_recv()
            passed[j].start()
        copy(0, sibling, me).wait_recv()
        for j, chip in enumerate(chips):
            copy(4 + j, (*chip, 1 - c), me).wait_recv()
        for cp in first + passed:
            cp.wait_send()
        mine.wait()

    return pl.pallas_call(
        body, name=name, out_shape=jax.ShapeDtypeStruct((N_DEV,) + x.shape, x.dtype),
        in_specs=[_HBM], out_specs=_HBM, scratch_shapes=_dma_sems(N_DEV - 1),
    )(x)


def sibling_exchange(g8, name):
    def body(x_ref, o_ref, send_sems, recv_sems, _):
        x, y, c = lax.axis_index("x"), lax.axis_index("y"), lax.axis_index("c")
        copies = [_remote(x_ref.at[2 * k + 1 - c], o_ref.at[k], send_sems.at[k], recv_sems.at[k], (x, y, 1 - c))
                  for k in range(N_CHIPS)]
        for cp in copies:
            cp.start()
        for cp in copies:
            cp.wait_recv()
        for cp in copies:
            cp.wait_send()

    return pl.pallas_call(
        body, name=name, out_shape=jax.ShapeDtypeStruct((N_CHIPS,) + g8.shape[1:], g8.dtype),
        in_specs=[_HBM], out_specs=_HBM, scratch_shapes=_dma_sems(N_CHIPS),
    )(g8)


def pair_add(g8, half, name):
    lead = g8.shape[1:]
    C = lead[-1]
    M = 1
    for d in lead[:-1]:
        M *= d
    TR = _row_tile(M, max(16, ADAM_BLOCK // C), 16)

    def body(c_ref, g_ref, h_ref, o_ref):
        o_ref[...] = (g_ref[...].astype(F32) + h_ref[...].astype(F32)).astype(BF16)

    blk = (1, TR, C)
    grid_spec = pltpu.PrefetchScalarGridSpec(
        num_scalar_prefetch=1, grid=(N_CHIPS, M // TR),
        in_specs=[pl.BlockSpec(blk, lambda k, i, c_ref: (2 * k + c_ref[0], i, 0)), pl.BlockSpec(blk, lambda k, i, c_ref: (k, i, 0))],
        out_specs=pl.BlockSpec(blk, lambda k, i, c_ref: (k, i, 0)))
    out = pl.pallas_call(
        body, name=name, grid_spec=grid_spec, out_shape=jax.ShapeDtypeStruct((N_CHIPS, M, C), BF16),
        compiler_params=_cp("parallel", "parallel"),
    )(lax.axis_index("c").astype(jnp.int32).reshape(1), g8.reshape(N_DEV, M, C), half.reshape(N_CHIPS, M, C))
    return out.reshape((N_CHIPS,) + lead)


def chip_exchange(t, name):
    def body(x_ref, o_ref, send_sems, recv_sems, local_sem):
        x, y, c = lax.axis_index("x"), lax.axis_index("y"), lax.axis_index("c")
        mychip = 2 * x + y
        chips = [(1 - x, y), (x, 1 - y), (1 - x, 1 - y)]
        mine = pltpu.make_async_copy(x_ref.at[mychip], o_ref.at[mychip], local_sem.at[0])
        mine.start()
        sends = []
        for j, (px, py) in enumerate(chips):
            cp = _remote(x_ref.at[2 * px + py], o_ref.at[mychip], send_sems.at[j], recv_sems.at[j], (px, py, c))
            cp.start()
            sends.append(cp)
        for j, (px, py) in enumerate(chips):
            _remote(x_ref.at[2 * px + py], o_ref.at[2 * px + py], send_sems.at[j], recv_sems.at[j], (px, py, c)).wait_recv()
        for cp in sends:
            cp.wait_send()
        mine.wait()

    return pl.pallas_call(
        body, name=name, out_shape=jax.ShapeDtypeStruct(t.shape, t.dtype),
        in_specs=[_HBM], out_specs=_HBM, scratch_shapes=_dma_sems(N_CHIPS - 1),
    )(t)


def reduce_exchange(g8, name):
    half = sibling_exchange(g8, "d2d_" + name)
    return chip_exchange(pair_add(g8, half, "pair_" + name), "ici_" + name)


def mm(a, b, *, mode, grid, a_blk, a_map, b_blk, b_map, o_shape, o_blk, o_map, o_dtype, name, into=None):
    nk = grid[2]
    na, nb, no = len(a_blk) - 2, len(b_blk) - 2, len(o_blk) - 2
    dims = _DIMS[mode]

    def body(*refs):
        if into is not None:
            a_ref, b_ref, _, o_ref, *scratch = refs
        else:
            a_ref, b_ref, o_ref, *scratch = refs
        p = _dot(_ld(a_ref, na), _ld(b_ref, nb), dims)
        if nk == 1:
            _st(o_ref, no, p.astype(o_dtype))
        else:
            acc = scratch[0]
            k = pl.program_id(2)

            @pl.when(k == 0)
            def _():
                acc[...] = p

            @pl.when(k > 0)
            def _():
                acc[...] += p

            @pl.when(k == nk - 1)
            def _():
                _st(o_ref, no, acc[...].astype(o_dtype))

    in_specs = [pl.BlockSpec(a_blk, a_map), pl.BlockSpec(b_blk, b_map)]
    args = [a, b]
    aliases = {}
    if into is not None:
        in_specs.append(pl.BlockSpec(memory_space=pl.ANY))
        args.append(into)
        aliases = {2: 0}
    return pl.pallas_call(
        body, name=name, grid=grid, out_shape=jax.ShapeDtypeStruct(o_shape, o_dtype),
        in_specs=in_specs, out_specs=pl.BlockSpec(o_blk, o_map),
        scratch_shapes=[pltpu.VMEM(tuple(o_blk[-2:]), F32)] if nk > 1 else [],
        input_output_aliases=aliases, compiler_params=_cp("parallel", "parallel", "arbitrary"),
    )(*args)


def proj_fwd(u, w8, o_dtype, name):
    S, D = u.shape
    nc = w8.shape[-1]
    T = min(MM_TILE, S)
    return mm(u, w8, mode="nn", grid=(S // T, N_DEV, 1), a_blk=(T, D), a_map=lambda i, j, k: (i, 0),
              b_blk=(1, 1, D, nc), b_map=lambda i, j, k: (j, 0, 0, 0), o_shape=(S, N_DEV * nc), o_blk=(T, nc),
              o_map=lambda i, j, k: (i, j), o_dtype=o_dtype, name=name)


def proj_bwd_x(dy, w8, name):
    S = dy.shape[0]
    D, nc = w8.shape[-2:]
    T = min(MM_TILE, S)
    return mm(dy, w8, mode="nt", grid=(S // T, 1, N_DEV), a_blk=(T, nc), a_map=lambda i, j, k: (i, k),
              b_blk=(1, 1, D, nc), b_map=lambda i, j, k: (k, 0, 0, 0), o_shape=(S, D), o_blk=(T, D),
              o_map=lambda i, j, k: (i, 0), o_dtype=F32, name=name)


def proj_bwd_w(u, dy, name):
    S, D = u.shape
    nc = dy.shape[1] // N_DEV
    T = min(MM_TILE, S)
    return mm(u, dy, mode="tn", grid=(N_DEV, 1, S // T), a_blk=(T, D), a_map=lambda j, _, k: (k, 0),
              b_blk=(T, nc), b_map=lambda j, _, k: (k, j), o_shape=(N_DEV, 1, D, nc), o_blk=(1, 1, D, nc),
              o_map=lambda j, _, k: (j, 0, 0, 0), o_dtype=BF16, name=name)


def dense_fwd(a, w, name):
    S, D = a.shape
    T = min(MM_TILE, S)
    return mm(a, w, mode="nn", grid=(S // T, 1, 1), a_blk=(T, D), a_map=lambda i, j, k: (i, 0), b_blk=(D, D),
              b_map=lambda i, j, k: (0, 0), o_shape=(S, D), o_blk=(T, D), o_map=lambda i, j, k: (i, 0),
              o_dtype=BF16, name=name)


def dense_bwd_x(dy, w, name):
    S, D = dy.shape
    T = min(MM_TILE, S)
    return mm(dy, w, mode="nt", grid=(S // T, 1, 1), a_blk=(T, D), a_map=lambda i, j, k: (i, 0), b_blk=(D, D),
              b_map=lambda i, j, k: (0, 0), o_shape=(S, D), o_blk=(T, D), o_map=lambda i, j, k: (i, 0),
              o_dtype=BF16, name=name)


def dense_bwd_w(a, dy, name):
    S, D = a.shape
    T = min(MM_TILE, S)
    half = D // 2
    return mm(a, dy, mode="tn", grid=(1, 2, S // T), a_blk=(T, D), a_map=lambda i, j, k: (k, 0), b_blk=(T, half),
              b_map=lambda i, j, k: (k, j), o_shape=(D, D), o_blk=(D, half), o_map=lambda i, j, k: (0, j),
              o_dtype=BF16, name=name)


def _mod_cols(nloc):
    return 256 if nloc % 256 == 0 else nloc


def mod_project(c_pad, w_mod, name):
    L, D, nloc = w_mod.shape
    R = c_pad.shape[0]
    tn = _mod_cols(nloc)

    def body(c_ref, w_ref, o_ref):
        c = c_ref[...]
        s = (c / (1.0 + jnp.exp(-c))).astype(BF16)
        o_ref[0] = _dot(s, w_ref[0].astype(BF16))

    return pl.pallas_call(
        body, name=name, grid=(L, nloc // tn), out_shape=jax.ShapeDtypeStruct((L, R, nloc), F32),
        in_specs=[pl.BlockSpec((R, D), lambda l, j: (0, 0)), pl.BlockSpec((1, D, tn), lambda l, j: (l, 0, j))],
        out_specs=pl.BlockSpec((1, R, tn), lambda l, j: (l, 0, j)), compiler_params=_cp("parallel", "parallel"),
    )(c_pad, w_mod)


def _adam(w, g, m, v):
    m = ADAM_B1 * m + (1.0 - ADAM_B1) * g
    v = ADAM_B2 * v + (1.0 - ADAM_B2) * (g * g)
    m_hat = m / (1.0 - ADAM_B1 ** ADAM_STEP)
    v_hat = v / (1.0 - ADAM_B2 ** ADAM_STEP)
    delta = -ADAM_LR * (m_hat / (jnp.sqrt(v_hat) + ADAM_EPS) + ADAM_WD * w)
    return delta, m, v


def mod_grad_adam(c_pad, dmod_pad, w, m, v, name):
    L, D, nloc = w.shape
    R = c_pad.shape[0]
    tn = _mod_cols(nloc)

    def body(c_ref, d_ref, w_ref, m_ref, v_ref, g_ref, dl_ref, nm_ref, nv_ref):
        c = c_ref[...]
        s = (c / (1.0 + jnp.exp(-c))).astype(BF16)
        g = _dot(s, d_ref[0].astype(BF16), _TN)
        delta, nm, nv = _adam(w_ref[0], g, m_ref[0], v_ref[0])
        g_ref[0], dl_ref[0], nm_ref[0], nv_ref[0] = g, delta, nm, nv

    wspec = pl.BlockSpec((1, D, tn), lambda l, j: (l, 0, j))
    out = jax.ShapeDtypeStruct(w.shape, F32)
    return pl.pallas_call(
        body, name=name, grid=(L, nloc // tn), out_shape=(out,) * 4,
        in_specs=[pl.BlockSpec((R, D), lambda l, j: (0, 0)), pl.BlockSpec((1, R, tn), lambda l, j: (l, 0, j)),
                  wspec, wspec, wspec],
        out_specs=(wspec,) * 4, compiler_params=_cp("parallel", "parallel"),
    )(c_pad, dmod_pad, w, m, v)


def resid_norm(h_prev, y_prev, gate_prev, gn, sc, sh, u_dtype, name):
    S, D = h_prev.shape
    T = min(ROW_TILE, S)
    has_res = y_prev is not None

    def body(*refs):
        if has_res:
            h_ref, y_ref, gate_ref, gn_ref, sc_ref, sh_ref, ho_ref, u_ref = refs
            h = h_ref[...] + gate_ref[...] * y_ref[...].astype(F32)
            ho_ref[...] = h
        else:
            h_ref, gn_ref, sc_ref, sh_ref, u_ref = refs
            h = h_ref[...]
        r = lax.rsqrt(jnp.mean(h * h, axis=-1, keepdims=True) + EPS)
        u = (h * r) * gn_ref[...] * (1.0 + sc_ref[...]) + sh_ref[...]
        u_ref[...] = u.astype(u_dtype)

    row = pl.BlockSpec((T, D), lambda i: (i, 0))
    vec = _vec_spec(D, 1)
    if has_res:
        h, u = pl.pallas_call(
            body, name=name, grid=(S // T,), out_shape=(jax.ShapeDtypeStruct((S, D), F32), jax.ShapeDtypeStruct((S, D), u_dtype)),
            in_specs=[row, row, vec, vec, vec, vec], out_specs=(row, row), compiler_params=_cp("parallel"),
        )(h_prev, y_prev, gate_prev, gn, sc, sh)
        return h, u
    u = pl.pallas_call(
        body, name=name, grid=(S // T,), out_shape=jax.ShapeDtypeStruct((S, D), u_dtype),
        in_specs=[row, vec, vec, vec], out_specs=row, compiler_params=_cp("parallel"),
    )(h_prev, gn, sc, sh)
    return h_prev, u


def gate_bwd(dh, y, gate, name):
    S, D = dh.shape
    T = min(ROW_TILE, S)

    def body(dh_ref, y_ref, gate_ref, dy_ref, dg_ref):
        i = pl.program_id(0)
        d = dh_ref[...]
        dy_ref[...] = (d * gate_ref[...]).astype(BF16)
        part = jnp.sum(d * y_ref[...].astype(F32), axis=0, keepdims=True)

        @pl.when(i == 0)
        def _():
            dg_ref[...] = part

        @pl.when(i > 0)
        def _():
            dg_ref[...] += part

    row = pl.BlockSpec((T, D), lambda i: (i, 0))
    vec = _vec_spec(D, 1)
    return pl.pallas_call(
        body, name=name, grid=(S // T,), out_shape=(jax.ShapeDtypeStruct((S, D), BF16), jax.ShapeDtypeStruct((1, D), F32)),
        in_specs=[row, row, vec], out_specs=(row, vec), compiler_params=_cp("arbitrary"),
    )(dh, y, gate)


def norm_bwd(du, h, dh_res, gn, sc, name):
    S, D = h.shape
    T = min(ROW_TILE, S)

    def body(du_ref, h_ref, dr_ref, gn_ref, sc_ref, dh_ref, dgn_ref, dsc_ref, dsh_ref):
        i = pl.program_id(0)
        d = du_ref[...].astype(F32)
        hh = h_ref[...]
        r = lax.rsqrt(jnp.mean(hh * hh, axis=-1, keepdims=True) + EPS)
        n = hh * r
        gn_v = gn_ref[...]
        dng = d * (1.0 + sc_ref[...])
        dn = dng * gn_v
        dh_ref[...] = dr_ref[...] + r * (dn - n * jnp.mean(dn * n, axis=-1, keepdims=True))
        parts = (jnp.sum(dng * n, axis=0, keepdims=True), jnp.sum(d * (n * gn_v), axis=0, keepdims=True),
                 jnp.sum(d, axis=0, keepdims=True))

        @pl.when(i == 0)
        def _():
            for ref, part in zip((dgn_ref, dsc_ref, dsh_ref), parts):
                ref[...] = part

        @pl.when(i > 0)
        def _():
            for ref, part in zip((dgn_ref, dsc_ref, dsh_ref), parts):
                ref[...] += part

    row = pl.BlockSpec((T, D), lambda i: (i, 0))
    vec = _vec_spec(D, 1)
    vshape = jax.ShapeDtypeStruct((1, D), F32)
    return pl.pallas_call(
        body, name=name, grid=(S // T,), out_shape=(jax.ShapeDtypeStruct((S, D), F32), vshape, vshape, vshape),
        in_specs=[row, row, row, vec, vec], out_specs=(row, vec, vec, vec), compiler_params=_cp("arbitrary"),
    )(du, h, dh_res, gn, sc)


def final_loss(h_prev, y_prev, gate_prev, fg, target, name):
    S, D = h_prev.shape
    T = min(ROW_TILE, S)

    def body(h_ref, y_ref, gate_ref, fg_ref, t_ref, dh_ref, dfg_ref, loss_ref):
        i = pl.program_id(0)
        h = h_ref[...] + gate_ref[...] * y_ref[...].astype(F32)
        r = lax.rsqrt(jnp.mean(h * h, axis=-1, keepdims=True) + EPS)
        n = h * r
        g = fg_ref[...]
        err = n * g - t_ref[...]
        dout = err * (1.0 / D)
        dn = dout * g
        dh_ref[...] = r * (dn - n * jnp.mean(dn * n, axis=-1, keepdims=True))
        dfg = jnp.sum(dout * n, axis=0, keepdims=True)
        part = 0.5 * jnp.sum(jnp.mean(err * err, axis=-1, keepdims=True), axis=0, keepdims=True)
        part = jnp.broadcast_to(part, (1, LANES))

        @pl.when(i == 0)
        def _():
            dfg_ref[...] = dfg
            loss_ref[...] = part

        @pl.when(i > 0)
        def _():
            dfg_ref[...] += dfg
            loss_ref[...] += part

    row = pl.BlockSpec((T, D), lambda i: (i, 0))
    vec = _vec_spec(D, 1)
    return pl.pallas_call(
        body, name=name, grid=(S // T,),
        out_shape=(jax.ShapeDtypeStruct((S, D), F32), jax.ShapeDtypeStruct((1, D), F32), jax.ShapeDtypeStruct((1, LANES), F32)),
        in_specs=[row, row, vec, vec, row], out_specs=(row, vec, pl.BlockSpec((1, LANES), lambda i: (0, 0))),
        compiler_params=_cp("arbitrary"),
    )(h_prev, y_prev, gate_prev, fg, target)


def _prev_halo(T):
    return lambda i: (jnp.maximum(i * (T // HALO) - 1, 0), 0)


def _next_halo(T, S):
    return lambda i: (jnp.minimum((i + 1) * (T // HALO), S // HALO - 1), 0)


def pool_fwd(u, w, ps, name):
    S, D = u.shape
    T = min(ROW_TILE, S)
    C = D // len(POOL_WINDOWS)

    def body(uc_ref, up_ref, w_ref, ps_ref, diff_ref, y_ref):
        i = pl.program_id(0)
        t = lax.broadcasted_iota(jnp.int32, (T, 1), 0) + i * T
        for g, win in enumerate(POOL_WINDOWS):
            cols = slice(g * C, (g + 1) * C)
            cur = uc_ref[:, cols]
            prev = jnp.where(i > 0, up_ref[:, cols], 0.0)
            s = jnp.concatenate([prev, cur], axis=0)
            k = 1
            while k < win:
                s = s + pltpu.roll(s, k, 0)
                k *= 2
            cnt = jnp.minimum(t + 1, win).astype(F32)
            diff = (s[HALO:, :] / cnt - cur).astype(BF16)
            diff_ref[:, cols] = diff
            y_ref[:, cols] = (_dot(diff, w_ref[g]) * ps_ref[:, cols]).astype(BF16)

    row = pl.BlockSpec((T, D), lambda i: (i, 0))
    out = jax.ShapeDtypeStruct((S, D), BF16)
    return pl.pallas_call(
        body, name=name, grid=(S // T,), out_shape=(out, out),
        in_specs=[row, pl.BlockSpec((HALO, D), _prev_halo(T)), pl.BlockSpec(w.shape, lambda i: (0, 0, 0)), _vec_spec(D, 1)],
        out_specs=(row, row), compiler_params=_cp("parallel"),
    )(u, u, w, ps)


def pool_bwd(dy, diff, w, ps, name):
    S, D = dy.shape
    T = min(ROW_TILE, S)
    G = len(POOL_WINDOWS)
    C = D // G

    def body(dy_ref, diff_ref, w_ref, ps_ref, dd_ref, dw_ref, dps_ref):
        i = pl.program_id(0)
        for g in range(G):
            cols = slice(g * C, (g + 1) * C)
            diff = diff_ref[:, cols]
            d = dy_ref[:, cols].astype(F32)
            ypre = _dot(diff, w_ref[g])
            dps = jnp.sum(d * ypre, axis=0, keepdims=True)
            dyp = (d * ps_ref[:, cols]).astype(BF16)
            dd_ref[:, cols] = _dot(dyp, w_ref[g], _NT)
            dw = _dot(diff, dyp, _TN)

            @pl.when(i == 0)
            def _():
                dw_ref[g] = dw
                dps_ref[:, cols] = dps

            @pl.when(i > 0)
            def _():
                dw_ref[g] += dw
                dps_ref[:, cols] += dps

    row = pl.BlockSpec((T, D), lambda i: (i, 0))
    wspec = pl.BlockSpec(w.shape, lambda i: (0, 0, 0))
    return pl.pallas_call(
        body, name=name, grid=(S // T,),
        out_shape=(jax.ShapeDtypeStruct((S, D), F32), jax.ShapeDtypeStruct(w.shape, F32), jax.ShapeDtypeStruct((1, D), F32)),
        in_specs=[row, row, wspec, _vec_spec(D, 1)], out_specs=(row, wspec, _vec_spec(D, 1)),
        compiler_params=_cp("arbitrary"),
    )(dy, diff, w, ps)


def pool_window_bwd(dd, name):
    S, D = dd.shape
    T = min(ROW_TILE, S)
    C = D // len(POOL_WINDOWS)
    n = T + HALO
    last = S // T - 1

    def body(dc_ref, dn_ref, du_ref):
        i = pl.program_id(0)
        t = lax.broadcasted_iota(jnp.int32, (n, 1), 0) + i * T
        for g, win in enumerate(POOL_WINDOWS):
            cols = slice(g * C, (g + 1) * C)
            cur = dc_ref[:, cols]
            nxt = jnp.where(i < last, dn_ref[:, cols], 0.0)
            cnt = jnp.minimum(t + 1, win).astype(F32)
            s = jnp.concatenate([cur, nxt], axis=0) / cnt
            k = 1
            while k < win:
                s = s + pltpu.roll(s, n - k, 0)
                k *= 2
            du_ref[:, cols] = s[:T, :] - cur

    row = pl.BlockSpec((T, D), lambda i: (i, 0))
    return pl.pallas_call(
        body, name=name, grid=(S // T,), out_shape=jax.ShapeDtypeStruct((S, D), F32),
        in_specs=[row, pl.BlockSpec((HALO, D), _next_halo(T, S))], out_specs=row, compiler_params=_cp("parallel"),
    )(dd, dd)


def _conv_chunk(D):
    return 512 if D % 512 == 0 else D


def conv_fwd(U, cw, name):
    S = U.shape[0]
    D = U.shape[1] // 3
    T = min(CONV_TILE, S)
    CH = _conv_chunk(D)

    def body(uc_ref, up_ref, cw_ref, q_ref):
        i = pl.program_id(0)
        for j in range(D // CH):
            cols = slice(j * CH, (j + 1) * CH)
            ccols = slice(D + j * CH, D + (j + 1) * CH)
            vcols = slice(2 * D + j * CH, 2 * D + (j + 1) * CH)
            zp = jnp.where(i > 0, up_ref[:, ccols] * up_ref[:, vcols], 0.0)
            z = jnp.concatenate([zp, uc_ref[:, ccols] * uc_ref[:, vcols]], axis=0)
            zc = cw_ref[2, :, cols] * z + cw_ref[1, :, cols] * pltpu.roll(z, 1, 0) + cw_ref[0, :, cols] * pltpu.roll(z, 2, 0)
            q_ref[:, cols] = (uc_ref[:, cols] * zc[HALO:, :]).astype(BF16)

    return pl.pallas_call(
        body, name=name, grid=(S // T,), out_shape=jax.ShapeDtypeStruct((S, D), BF16),
        in_specs=[pl.BlockSpec((T, 3 * D), lambda i: (i, 0)), pl.BlockSpec((HALO, 3 * D), _prev_halo(T)),
                  pl.BlockSpec((3, 1, D), lambda i: (0, 0, 0))],
        out_specs=pl.BlockSpec((T, D), lambda i: (i, 0)), compiler_params=_cp("parallel"),
    )(U, U, cw)


def conv_bwd(dq, U, cw, name):
    S = U.shape[0]
    D = U.shape[1] // 3
    T = min(CONV_TILE, S)
    CH = _conv_chunk(D)
    n = T + HALO
    last = S // T - 1

    def body(dq_ref, dqn_ref, uc_ref, up_ref, un_ref, cw_ref, du_ref, dcw_ref):
        i = pl.program_id(0)
        for j in range(D // CH):
            cols = slice(j * CH, (j + 1) * CH)
            ccols = slice(D + j * CH, D + (j + 1) * CH)
            vcols = slice(2 * D + j * CH, 2 * D + (j + 1) * CH)
            w0, w1, w2 = cw_ref[0, :, cols], cw_ref[1, :, cols], cw_ref[2, :, cols]
            b, c, v = uc_ref[:, cols], uc_ref[:, ccols], uc_ref[:, vcols]
            dqc = dq_ref[:, cols].astype(F32)
            zp = jnp.where(i > 0, up_ref[:, ccols] * up_ref[:, vcols], 0.0)
            z = jnp.concatenate([zp, c * v], axis=0)
            z1 = pltpu.roll(z, 1, 0)[HALO:, :]
            z2 = pltpu.roll(z, 2, 0)[HALO:, :]
            z0 = z[HALO:, :]
            zc = w2 * z0 + w1 * z1 + w0 * z2
            dzc = dqc * b
            dzn = jnp.where(i < last, dqn_ref[:, cols].astype(F32) * un_ref[:, cols], 0.0)
            e = jnp.concatenate([dzc, dzn], axis=0)
            dz = (w2 * e + w1 * pltpu.roll(e, n - 1, 0) + w0 * pltpu.roll(e, n - 2, 0))[:T, :]
            du_ref[:, cols] = (dqc * zc).astype(BF16)
            du_ref[:, ccols] = (dz * v).astype(BF16)
            du_ref[:, vcols] = (dz * c).astype(BF16)
            parts = [jnp.sum(dzc * zz, axis=0, keepdims=True) for zz in (z2, z1, z0)]

            @pl.when(i == 0)
            def _():
                for k in range(3):
                    dcw_ref[k, :, cols] = parts[k]

            @pl.when(i > 0)
            def _():
                for k in range(3):
                    dcw_ref[k, :, cols] += parts[k]

    return pl.pallas_call(
        body, name=name, grid=(S // T,),
        out_shape=(jax.ShapeDtypeStruct((S, 3 * D), BF16), jax.ShapeDtypeStruct((3, 1, D), F32)),
        in_specs=[pl.BlockSpec((T, D), lambda i: (i, 0)), pl.BlockSpec((HALO, D), _next_halo(T, S)),
                  pl.BlockSpec((T, 3 * D), lambda i: (i, 0)), pl.BlockSpec((HALO, 3 * D), _prev_halo(T)),
                  pl.BlockSpec((HALO, 3 * D), _next_halo(T, S)), pl.BlockSpec((3, 1, D), lambda i: (0, 0, 0))],
        out_specs=(pl.BlockSpec((T, 3 * D), lambda i: (i, 0)), pl.BlockSpec((3, 1, D), lambda i: (0, 0, 0))),
        compiler_params=_cp("arbitrary"),
    )(dq, dq, U, U, U, cw)


MASKED_LOG = -1e30


def _sb_logits(z, q0, k0, masked):
    z = z * (HEAD_DIM ** -0.5)
    lb = jnp.minimum(z, 0.0) - jnp.log(1.0 + jnp.exp(-jnp.abs(z)))
    l1 = lb - z
    if masked:
        causal = (lax.broadcasted_iota(jnp.int32, z.shape, 1) + k0) < (lax.broadcasted_iota(jnp.int32, z.shape, 0) + q0)
        lb = jnp.where(causal, lb, MASKED_LOG)
        l1 = jnp.where(causal, l1, 0.0)
    return lb, l1


def _key_sums(l1, tri):
    hi = l1.astype(BF16)
    lo = (l1 - hi.astype(F32)).astype(BF16)
    return _dot(hi, tri) + _dot(lo, tri)


def _tri(B, cmp):
    row = lax.broadcasted_iota(jnp.int32, (B, B), 0)
    col = lax.broadcasted_iota(jnp.int32, (B, B), 1)
    return cmp(row, col).astype(BF16)


def attn_fwd(qkv, name):
    S = qkv.shape[0]
    D = qkv.shape[1] // 3
    H = D // HEAD_DIM
    BK = min(ATT_K, S)
    BQ = min(ATT_Q, S)
    ND = BQ // BK

    def body(q_ref, k_ref, v_ref, o_ref, lt_ref):
        later = _tri(BK, lambda j, s: j > s)

        def q_loop(qi, _):
            q0 = pl.multiple_of(qi * BQ, BQ)
            qb = q_ref[pl.ds(q0, BQ), :]
            n_off = qi * ND

            def scores(kb):
                k0 = pl.multiple_of(kb * BK, BK)
                return _dot(qb, k_ref[pl.ds(k0, BK), :], _NT)

            def accumulate(kb, z, masked, carry, oacc):
                k0 = pl.multiple_of(kb * BK, BK)
                lb, l1 = _sb_logits(z, q0, k0, masked)
                suffix = _key_sums(l1, later) + carry
                a = jnp.exp(lb + suffix)
                oacc = oacc + _dot(a.astype(BF16), v_ref[pl.ds(k0, BK), :])
                return carry + jnp.sum(l1, axis=1, keepdims=True), oacc

            carry, oacc = jnp.zeros((BQ, 1), F32), jnp.zeros((BQ, HEAD_DIM), F32)
            z = scores(n_off + ND - 1)
            for d in reversed(range(ND)):
                z_next = scores(jnp.maximum(n_off + d - 1, 0))
                carry, oacc = accumulate(n_off + d, z, True, carry, oacc)
                z = z_next

            def k_loop(t, state):
                carry, oacc, z = state
                kb = n_off - 1 - t
                z_next = scores(jnp.maximum(kb - 1, 0))
                carry, oacc = accumulate(kb, z, False, carry, oacc)
                return carry, oacc, z_next

            total, oacc, _ = lax.fori_loop(0, n_off, k_loop, (carry, oacc, z))
            o_ref[pl.ds(q0, BQ), :] = oacc.astype(BF16)
            lt_ref[pl.ds(q0, BQ), :] = jnp.broadcast_to(total, (BQ, HEAD_DIM))
            return 0

        lax.fori_loop(0, S // BQ, q_loop, 0)

    def head(off):
        return pl.BlockSpec((S, HEAD_DIM), lambda h: (0, off + h))

    return pl.pallas_call(
        body, name=name, grid=(H,), out_shape=(jax.ShapeDtypeStruct((S, D), BF16), jax.ShapeDtypeStruct((S, D), F32)),
        in_specs=[head(0), head(H), head(2 * H)], out_specs=(head(0), head(0)), compiler_params=_cp("parallel"),
    )(qkv, qkv, qkv)


def attn_bwd(qkv, lt, do, name):
    S = qkv.shape[0]
    D = qkv.shape[1] // 3
    H = D // HEAD_DIM
    BK = min(ATT_K, S)
    BQ = min(ATT_Q_BWD, S)
    ND = BQ // BK
    scale = HEAD_DIM ** -0.5

    def body(q_ref, k_ref, v_ref, lt_ref, do_ref, dq_ref, dk_ref, dv_ref, dk_acc, dv_acc):
        upto = _tri(BK, lambda j, s: j <= s)
        before = _tri(BK, lambda j, s: j < s)
        dk_acc[...] = jnp.zeros_like(dk_acc)
        dv_acc[...] = jnp.zeros_like(dv_acc)

        def q_loop(qi, _):
            q0 = pl.multiple_of(qi * BQ, BQ)
            qb = q_ref[pl.ds(q0, BQ), :]
            dob = do_ref[pl.ds(q0, BQ), :]
            total = lt_ref[pl.ds(q0, BQ), :][:, :1]
            n_off = qi * ND

            def scores(kb):
                k0 = pl.multiple_of(kb * BK, BK)
                return _dot(qb, k_ref[pl.ds(k0, BK), :], _NT), _dot(dob, v_ref[pl.ds(k0, BK), :], _NT)

            def accumulate(kb, z, da, masked, lsum, gsum, dqacc):
                k0 = pl.multiple_of(kb * BK, BK)
                lb, l1 = _sb_logits(z, q0, k0, masked)
                suffix = total - lsum - _key_sums(l1, upto)
                a = jnp.exp(lb + suffix)
                g = a * da
                gpre = gsum + _dot(g.astype(BF16), before)
                beta = jnp.exp(lb)
                dz = ((g * (1.0 - beta) - gpre * beta) * scale).astype(BF16)
                dqacc = dqacc + _dot(dz, k_ref[pl.ds(k0, BK), :])
                dk_acc[pl.ds(k0, BK), :] += _dot(dz, qb, _TN)
                dv_acc[pl.ds(k0, BK), :] += _dot(a.astype(BF16), dob, _TN)
                return (lsum + jnp.sum(l1, axis=1, keepdims=True), gsum + jnp.sum(g, axis=1, keepdims=True), dqacc)

            def k_loop(t, state):
                lsum, gsum, dqacc, z, da = state
                z_next, da_next = scores(t + 1)
                lsum, gsum, dqacc = accumulate(t, z, da, False, lsum, gsum, dqacc)
                return lsum, gsum, dqacc, z_next, da_next

            zero = jnp.zeros((BQ, 1), F32)
            lsum, gsum, dqacc, z, da = lax.fori_loop(
                0, n_off, k_loop, (zero, zero, jnp.zeros((BQ, HEAD_DIM), F32), *scores(0)))
            for d in range(ND):
                if d + 1 < ND:
                    z_next, da_next = scores(n_off + d + 1)
                lsum, gsum, dqacc = accumulate(n_off + d, z, da, True, lsum, gsum, dqacc)
                z, da = z_next, da_next
            dq_ref[pl.ds(q0, BQ), :] = dqacc.astype(BF16)
            return 0

        lax.fori_loop(0, S // BQ, q_loop, 0)
        dk_ref[...] = dk_acc[...].astype(BF16)
        dv_ref[...] = dv_acc[...].astype(BF16)

    def head(off):
        return pl.BlockSpec((S, HEAD_DIM), lambda h: (0, off + h))

    out = jax.ShapeDtypeStruct((S, D), BF16)
    return pl.pallas_call(
        body, name=name, grid=(H,), out_shape=(out, out, out),
        in_specs=[head(0), head(H), head(2 * H), head(0), head(0)], out_specs=(head(0), head(0), head(0)),
        scratch_shapes=[pltpu.VMEM((S, HEAD_DIM), F32), pltpu.VMEM((S, HEAD_DIM), F32)], compiler_params=_cp("parallel"),
    )(qkv, qkv, qkv, lt, do)


def ffn_fwd(u, wgu8, wd8, l, name):
    S, D = u.shape
    n = wgu8.shape[-1]
    T = min(FFN_TILE, S)

    def body(u_ref, wgu_ref, wd_ref, gp_ref, y_ref, acc):
        j = pl.program_id(1)
        uu = u_ref[...]
        g = _dot(uu, wgu_ref[0, 0, 0])
        p = _dot(uu, wgu_ref[0, 1, 0])
        gp_ref[0, 0] = g.astype(BF16)
        gp_ref[1, 0] = p.astype(BF16)
        a = (g / (1.0 + jnp.exp(-g)) * p).astype(BF16)
        part = _dot(a, wd_ref[0, 0])

        @pl.when(j == 0)
        def _():
            acc[...] = part

        @pl.when(j > 0)
        def _():
            acc[...] += part

        @pl.when(j == N_DEV - 1)
        def _():
            y_ref[...] = acc[...].astype(BF16)

    return pl.pallas_call(
        body, name=name, grid=(S // T, N_DEV),
        out_shape=(jax.ShapeDtypeStruct((2, N_DEV, S, n), BF16), jax.ShapeDtypeStruct((S, D), BF16)),
        in_specs=[pl.BlockSpec((T, D), lambda i, j: (i, 0)), pl.BlockSpec((1, 2, 1, D, n), lambda i, j: (j, 0, l, 0, 0)),
                  pl.BlockSpec((1, 1, n, D), lambda i, j: (j, l, 0, 0))],
        out_specs=(pl.BlockSpec((2, 1, T, n), lambda i, j: (0, j, i, 0)), pl.BlockSpec((T, D), lambda i, j: (i, 0))),
        scratch_shapes=[pltpu.VMEM((T, D), F32)], compiler_params=_cp("parallel", "arbitrary"),
    )(u, wgu8, wd8)


def ffn_bwd_act(dy, gp, wd8, l, name):
    S, D = dy.shape
    n = gp.shape[-1]
    T = min(FFN_TILE, S)

    def body(dy_ref, gp_ref, wd_ref, a_ref, dgp_ref):
        da = _dot(dy_ref[...], wd_ref[0, 0], _NT)
        g = gp_ref[0, 0].astype(F32)
        p = gp_ref[1, 0].astype(F32)
        s = 1.0 / (1.0 + jnp.exp(-g))
        sl = g * s
        a_ref[0] = (sl * p).astype(BF16)
        dgp_ref[0, 0] = (da * p * (s * (1.0 + g * (1.0 - s)))).astype(BF16)
        dgp_ref[1, 0] = (da * sl).astype(BF16)

    gp_spec = pl.BlockSpec((2, 1, T, n), lambda i, j: (0, j, i, 0))
    return pl.pallas_call(
        body, name=name, grid=(S // T, N_DEV),
        out_shape=(jax.ShapeDtypeStruct((N_DEV, S, n), BF16), jax.ShapeDtypeStruct((2, N_DEV, S, n), BF16)),
        in_specs=[pl.BlockSpec((T, D), lambda i, j: (i, 0)), gp_spec, pl.BlockSpec((1, 1, n, D), lambda i, j: (j, l, 0, 0))],
        out_specs=(pl.BlockSpec((1, T, n), lambda i, j: (j, i, 0)), gp_spec), compiler_params=_cp("parallel", "parallel"),
    )(dy, gp, wd8)


def ffn_bwd_x(dgp, wgu8, l, name):
    S, n = dgp.shape[-2:]
    D = wgu8.shape[-2]
    T = min(MM_TILE, S)
    return mm(dgp, wgu8, mode="nt", grid=(S // T, 1, 2 * N_DEV), a_blk=(1, 1, T, n),
              a_map=lambda i, j, k: (k // N_DEV, k % N_DEV, i, 0), b_blk=(1, 1, 1, D, n),
              b_map=lambda i, j, k: (k % N_DEV, k // N_DEV, l, 0, 0), o_shape=(S, D), o_blk=(T, D),
              o_map=lambda i, j, k: (i, 0), o_dtype=F32, name=name)


def ffn_bwd_wgu(u, dgp, into, l, name):
    S, D = u.shape
    n = dgp.shape[-1]
    T = min(MM_TILE, S)
    return mm(u, dgp, mode="tn", grid=(N_DEV, 2, S // T), a_blk=(T, D), a_map=lambda j, t, k: (k, 0),
              b_blk=(1, 1, T, n), b_map=lambda j, t, k: (t, j, k, 0), o_shape=into.shape, o_blk=(1, 1, 1, D, n),
              o_map=lambda j, t, k: (j, t, l, 0, 0), o_dtype=BF16, name=name, into=into)


def ffn_bwd_wd(a8, dy, into, l, name):
    n = a8.shape[-1]
    S, D = dy.shape
    T = min(MM_TILE, S)
    return mm(a8, dy, mode="tn", grid=(N_DEV, 1, S // T), a_blk=(1, T, n), a_map=lambda j, _, k: (j, k, 0),
              b_blk=(T, D), b_map=lambda j, _, k: (k, 0), o_shape=into.shape, o_blk=(1, 1, n, D),
              o_map=lambda j, _, k: (j, l, 0, 0), o_dtype=BF16, name=name, into=into)


def adam_update(recv, a_idx, w, m, v, name):
    L, R, C = w.shape
    TR = _row_tile(R, max(8, ADAM_BLOCK // C))
    nsrc = recv.shape[0]

    def body(r_ref, w_ref, m_ref, v_ref, g_ref, d_ref, nm_ref, nv_ref):
        g = r_ref[0, 0, 0].astype(F32)
        for k in range(1, nsrc):
            g = g + r_ref[k, 0, 0].astype(F32)
        delta, nm, nv = _adam(w_ref[0], g, m_ref[0], v_ref[0])
        g_ref[0], d_ref[0], nm_ref[0], nv_ref[0] = g, delta, nm, nv

    wspec = pl.BlockSpec((1, TR, C), lambda l, i: (l, i, 0))
    out = jax.ShapeDtypeStruct(w.shape, F32)
    return pl.pallas_call(
        body, name=name, grid=(L, R // TR), out_shape=(out,) * 4,
        in_specs=[pl.BlockSpec((nsrc, 1, 1, TR, C), lambda l, i: (0, a_idx, l, i, 0)), wspec, wspec, wspec],
        out_specs=(wspec,) * 4, compiler_params=_cp("parallel", "parallel"),
    )(recv, w, m, v)


def sum_devices(x, name):
    _, R, C = x.shape

    def body(x_ref, o_ref):
        s = x_ref[0]
        for k in range(1, N_DEV):
            s = s + x_ref[k]
        o_ref[...] = s

    return pl.pallas_call(
        body, name=name, out_shape=jax.ShapeDtypeStruct((R, C), F32),
        in_specs=[pl.BlockSpec(memory_space=pltpu.VMEM)], out_specs=pl.BlockSpec(memory_space=pltpu.VMEM),
    )(x)


def _pack(parts):
    flat = jnp.concatenate([p.reshape(-1) for p in parts])
    pad = (-flat.shape[0]) % (8 * LANES)
    return jnp.pad(flat, (0, pad)).reshape(-1, LANES)


def _unpack(packed, shapes, lead=()):
    flat = packed.reshape(lead + (-1,))
    out, off = [], 0
    for s in shapes:
        size = 1
        for d in s:
            size *= d
        out.append(flat[..., off:off + size].reshape(lead + tuple(s)))
        off += size
    return out


def kernel(x, c, norm_mix_g, norm_ffn_g, w_mod, b_mod, pool_w, pool_scale, conv_w_in, conv_w, conv_w_out, sb_w_qkv, sb_w_o, ffn_w_gate, ffn_w_up, ffn_w_down, final_g, loss_target, m_norm_mix_g, m_norm_ffn_g, m_w_mod, m_b_mod, m_pool_w, m_pool_scale, m_conv_w_in, m_conv_w, m_conv_w_out, m_sb_w_qkv, m_sb_w_o, m_ffn_w_gate, m_ffn_w_up, m_ffn_w_down, m_final_g, v_norm_mix_g, v_norm_ffn_g, v_w_mod, v_b_mod, v_pool_w, v_pool_scale, v_conv_w_in, v_conv_w, v_conv_w_out, v_sb_w_qkv, v_sb_w_o, v_ffn_w_gate, v_ffn_w_up, v_ffn_w_down, v_final_g):
    S, D = x.shape[1:]
    L = N_LAYERS
    G = len(POOL_WINDOWS)
    C = D // G
    dloc = D // N_DEV
    nmod = w_mod.shape[-1]
    me = _my_index()

    wgu8 = all_gather(jnp.stack([ffn_w_gate, ffn_w_up]).astype(BF16), "ag_ffn_gate_up")
    wd8 = all_gather(ffn_w_down.astype(BF16), "ag_ffn_down")
    wcin8 = all_gather(conv_w_in.astype(BF16), "ag_conv_in")
    wcout = all_gather(conv_w_out.astype(BF16), "ag_conv_out").reshape(D, D)
    wqkv8 = all_gather(sb_w_qkv.astype(BF16), "ag_sb_qkv")
    wo = all_gather(sb_w_o.astype(BF16), "ag_sb_o").reshape(D, D)
    pw = all_gather(pool_w.astype(BF16), "ag_pool_w")
    pw = pw.transpose(1, 2, 0, 3, 4).reshape(pool_w.shape[0], G, C, C)
    small_shapes = [(1, D), (3, dloc), (pool_scale.shape[0], dloc)]
    small = all_gather(_pack([c, conv_w, pool_scale]), "ag_small_in")
    c_all, cw_all, ps_all = _unpack(small, small_shapes, (N_DEV,))
    c_pad = jnp.pad(c_all.reshape(N_DEV, D), ((0, N_DEV), (0, 0)))
    cw = cw_all.transpose(1, 0, 2).reshape(3, 1, D)
    ps = ps_all.transpose(1, 0, 2).reshape(-1, D)

    mod_part = mod_project(c_pad, w_mod, "mod_project")
    mod_all = all_gather(mod_part, "ag_mod")
    mod = lax.dynamic_index_in_dim(mod_all, me, axis=2, keepdims=False)
    mod = mod.transpose(1, 0, 2).reshape(L, N_MOD * D) + b_mod

    def mod_vec(i, k):
        return mod[i, k * D:(k + 1) * D].reshape(1, D)

    h = x[0]
    y_prev = gate_prev = None
    saved = []
    for i in range(L):
        sh_m, sc_m, g_m, sh_f, sc_f, g_f = (mod_vec(i, k) for k in range(N_MOD))
        kind, j = i % 3, i // 3
        gn_m, gn_f = norm_mix_g[i].reshape(1, D), norm_ffn_g[i].reshape(1, D)
        h_mix, u = resid_norm(h, y_prev, gate_prev, gn_m, sc_m, sh_m, F32 if kind == 0 else BF16, f"norm_mix_{i}")
        if kind == 0:
            diff, y = pool_fwd(u, pw[j], ps[j].reshape(1, D), f"pool_fwd_{i}")
            mix = (diff,)
        elif kind == 1:
            U = proj_fwd(u, wcin8, F32, f"conv_in_{i}")
            q = conv_fwd(U, cw, f"conv_fwd_{i}")
            y = dense_fwd(q, wcout, f"conv_out_{i}")
            mix = (u, U, q)
        else:
            qkv = proj_fwd(u, wqkv8, BF16, f"sb_qkv_{i}")
            o, lt = attn_fwd(qkv, f"attn_fwd_{i}")
            y = dense_fwd(o, wo, f"sb_out_{i}")
            mix = (u, qkv, o, lt)
        h_ffn, u_ffn = resid_norm(h_mix, y, g_m, gn_f, sc_f, sh_f, BF16, f"norm_ffn_{i}")
        gp, y_ffn = ffn_fwd(u_ffn, wgu8, wd8, i, f"ffn_fwd_{i}")
        saved.append((h_mix, mix, y, h_ffn, u_ffn, gp, y_ffn))
        h, y_prev, gate_prev = h_ffn, y_ffn, g_f

    dh, dfg, loss_part = final_loss(h, y_prev, gate_prev, final_g.reshape(1, D), loss_target[0], "final_loss")
    loss = lax.psum(loss_part[0, 0], ("x", "y", "c"))

    dwgu8 = jnp.zeros(wgu8.shape, BF16)
    dwd8 = jnp.zeros(wd8.shape, BF16)
    dpw, dps, dmod, dgn_mix, dgn_ffn = {}, {}, [None] * L, [None] * L, [None] * L
    for i in reversed(range(L)):
        sh_m, sc_m, g_m, sh_f, sc_f, g_f = (mod_vec(i, k) for k in range(N_MOD))
        kind, j = i % 3, i // 3
        gn_m, gn_f = norm_mix_g[i].reshape(1, D), norm_ffn_g[i].reshape(1, D)
        h_mix, mix, y, h_ffn, u_ffn, gp, y_ffn = saved[i]

        dy, dg_f = gate_bwd(dh, y_ffn, g_f, f"gate_bwd_ffn_{i}")
        a8, dgp = ffn_bwd_act(dy, gp, wd8, i, f"ffn_bwd_act_{i}")
        du = ffn_bwd_x(dgp, wgu8, i, f"ffn_bwd_x_{i}")
        dwgu8 = ffn_bwd_wgu(u_ffn, dgp, dwgu8, i, f"ffn_bwd_wgu_{i}")
        dwd8 = ffn_bwd_wd(a8, dy, dwd8, i, f"ffn_bwd_wd_{i}")
        dh, dgn_ffn[i], dsc_f, dsh_f = norm_bwd(du, h_ffn, dh, gn_f, sc_f, f"norm_bwd_ffn_{i}")

        dy, dg_m = gate_bwd(dh, y, g_m, f"gate_bwd_mix_{i}")
        if kind == 0:
            (diff,) = mix
            dd, dpw[j], dps[j] = pool_bwd(dy, diff, pw[j], ps[j].reshape(1, D), f"pool_bwd_{i}")
            du = pool_window_bwd(dd, f"pool_window_bwd_{i}")
        elif kind == 1:
            u, U, q = mix
            dq = dense_bwd_x(dy, wcout, f"conv_out_bwd_x_{i}")
            dwcout = dense_bwd_w(q, dy, f"conv_out_bwd_w_{i}")
            dU, dcw = conv_bwd(dq, U, cw, f"conv_bwd_{i}")
            du = proj_bwd_x(dU, wcin8, f"conv_in_bwd_x_{i}")
            dwcin8 = proj_bwd_w(u, dU, f"conv_in_bwd_w_{i}")
        else:
            u, qkv, o, lt = mix
            do = dense_bwd_x(dy, wo, f"sb_out_bwd_x_{i}")
            dwo = dense_bwd_w(o, dy, f"sb_out_bwd_w_{i}")
            dq_, dk_, dv_ = attn_bwd(qkv, lt, do, f"attn_bwd_{i}")
            dqkv = jnp.concatenate([dq_, dk_, dv_], axis=1)
            du = proj_bwd_x(dqkv, wqkv8, f"sb_qkv_bwd_x_{i}")
            dwqkv8 = proj_bwd_w(u, dqkv, f"sb_qkv_bwd_w_{i}")
        dh, dgn_mix[i], dsc_m, dsh_m = norm_bwd(du, h_mix, dh, gn_m, sc_m, f"norm_bwd_mix_{i}")
        dmod[i] = jnp.concatenate([dsh_m, dsc_m, dg_m, dsh_f, dsc_f, dg_f], axis=1)
    grad_x = dh[None]

    def reduce_update(g8, w, m_, v_, name):
        recv = reduce_exchange(g8, name)
        w3 = w.reshape((-1,) + w.shape[-2:])
        recv = recv.reshape((N_CHIPS, 1) + w3.shape)
        outs = adam_update(recv, 0, w3, m_.reshape(w3.shape), v_.reshape(w3.shape), "adam_" + name)
        return [o_.reshape(w.shape) for o_ in outs]

    recv_gu = reduce_exchange(dwgu8, "ffn_gate_up")
    up_gate = adam_update(recv_gu, 0, ffn_w_gate, m_ffn_w_gate, v_ffn_w_gate, "adam_ffn_gate")
    up_up = adam_update(recv_gu, 1, ffn_w_up, m_ffn_w_up, v_ffn_w_up, "adam_ffn_up")
    up_down = reduce_update(dwd8, ffn_w_down, m_ffn_w_down, v_ffn_w_down, "ffn_down")
    up_cin = reduce_update(dwcin8, conv_w_in, m_conv_w_in, v_conv_w_in, "conv_in")
    up_cout = reduce_update(dwcout.reshape(N_DEV, 1, dloc, D), conv_w_out, m_conv_w_out, v_conv_w_out, "conv_out")
    up_qkv = reduce_update(dwqkv8, sb_w_qkv, m_sb_w_qkv, v_sb_w_qkv, "sb_qkv")
    up_o = reduce_update(dwo.reshape(N_DEV, 1, dloc, D), sb_w_o, m_sb_w_o, v_sb_w_o, "sb_o")
    dpw_all = jnp.stack([dpw[j] for j in range(pool_w.shape[0])])
    dpw8 = dpw_all.reshape(-1, G, N_DEV, C // N_DEV, C).transpose(2, 0, 1, 3, 4).astype(BF16)
    up_pool = reduce_update(dpw8, pool_w, m_pool_w, v_pool_w, "pool_w")

    dps_all = jnp.concatenate([dps[j] for j in range(pool_scale.shape[0])], axis=0)
    dmod_loc = jnp.concatenate(dmod, axis=0)
    part_shapes = [(L, D), (L, D), (1, D), (L, N_MOD * D), dps_all.shape, (3, 1, D)]
    parts8 = all_gather(_pack([jnp.concatenate(dgn_mix, axis=0), jnp.concatenate(dgn_ffn, axis=0), dfg, dmod_loc,
                               dps_all, dcw]), "ag_small_out")
    g_mix, g_ffn, g_fin, g_bmod, g_ps, g_cw = _unpack(sum_devices(parts8, "sum_small"), part_shapes)
    g_ps = lax.dynamic_slice_in_dim(g_ps, me * dloc, dloc, axis=1)
    g_cw = lax.dynamic_slice_in_dim(g_cw.reshape(3, D), me * dloc, dloc, axis=1)
    dmod_all = _unpack(parts8, part_shapes, (N_DEV,))[3]
    dmod_cols = lax.dynamic_slice_in_dim(dmod_all, me * nmod, nmod, axis=2)
    dmod_pad = jnp.pad(dmod_cols.transpose(1, 0, 2), ((0, 0), (0, N_DEV), (0, 0)))
    up_wmod = mod_grad_adam(c_pad, dmod_pad, w_mod, m_w_mod, v_w_mod, "mod_grad_adam")

    small_w = [norm_mix_g, norm_ffn_g, final_g, b_mod, pool_scale, conv_w]
    small_m = [m_norm_mix_g, m_norm_ffn_g, m_final_g, m_b_mod, m_pool_scale, m_conv_w]
    small_v = [v_norm_mix_g, v_norm_ffn_g, v_final_g, v_b_mod, v_pool_scale, v_conv_w]
    small_g = [g_mix, g_ffn, g_fin.reshape(final_g.shape), g_bmod, g_ps, g_cw.reshape(conv_w.shape)]
    packed = [_pack(t)[None] for t in (small_w, small_m, small_v)]
    up_small = adam_update(_pack(small_g)[None, None, None], 0, *packed, "adam_small")
    sshapes = [t.shape for t in small_w]
    up_small = [_unpack(t[0], sshapes) for t in up_small]
    (s_mix, s_ffn, s_fin, s_bmod, s_ps, s_cw) = zip(*up_small)

    per_weight = [s_mix, s_ffn, up_wmod, s_bmod, up_pool, s_ps, up_cin, s_cw, up_cout, up_qkv, up_o,
                  up_gate, up_up, up_down, s_fin]
    outs = [loss, grad_x]
    for kind_idx in range(4):
        outs.extend(t[kind_idx] for t in per_weight)
    return tuple(outs)
```

```python
import functools

import jax
import jax.numpy as jnp
from jax import lax
from jax.experimental import pallas as pl
from jax.experimental.pallas import tpu as pltpu

F32, BF16 = jnp.float32, jnp.bfloat16
MESH = pl.DeviceIdType.MESH
N_DEV = 8
N_LAYERS = 4
N_MOD = 6
HEAD_DIM = 128
POOL_WINDOWS = (2, 4, 8, 16)
EPS = 1e-6
ADAM_LR, ADAM_B1, ADAM_B2, ADAM_EPS, ADAM_WD, ADAM_STEP = 0.001, 0.9, 0.999, 1e-08, 0.01, 10

LANES = 128
HALO = 16
ROW_TILE = 512
CONV_TILE = 256
MM_TILE = 1024
FFN_TILE = 512
ATT_K = 256
ATT_Q = 1024
ATT_Q_BWD = 512
ADAM_BLOCK = 256 * 1024
VMEM_LIMIT = 56 * 1024 * 1024

_NN = (((1,), (0,)), ((), ()))
_NT = (((1,), (1,)), ((), ()))
_TN = (((0,), (0,)), ((), ()))
_DIMS = {"nn": _NN, "nt": _NT, "tn": _TN}


def _cp(*sem):
    return pltpu.CompilerParams(dimension_semantics=sem if sem else None, vmem_limit_bytes=VMEM_LIMIT)


def _dot(a, b, dims=_NN):
    return lax.dot_general(a, b, dims, preferred_element_type=F32)


def _ld(ref, nlead):
    return ref[...] if nlead == 0 else ref[(0,) * nlead]


def _st(ref, nlead, val):
    if nlead == 0:
        ref[...] = val
    else:
        ref[(0,) * nlead] = val


def _row_tile(rows, cap, mult=8):
    if rows <= cap:
        return rows
    t = cap - cap % mult
    while rows % t:
        t -= mult
    return t


def _vec_spec(d, nidx):
    zero = (0, 0)
    return pl.BlockSpec((1, d), {1: lambda i: zero, 2: lambda i, j: zero}[nidx])


def _my_index():
    return 4 * lax.axis_index("x") + 2 * lax.axis_index("y") + lax.axis_index("c")


_HBM = pl.BlockSpec(memory_space=pl.ANY)
N_CHIPS = N_DEV // 2


def _remote(src, dst, send_sem, recv_sem, to):
    return pltpu.make_async_remote_copy(src_ref=src, dst_ref=dst, send_sem=send_sem, recv_sem=recv_sem,
                                        device_id=to, device_id_type=MESH)


def _dma_sems(n):
    return [pltpu.SemaphoreType.DMA((n,)), pltpu.SemaphoreType.DMA((n,)), pltpu.SemaphoreType.DMA((1,))]


def all_gather(x, name):
    def body(x_ref, o_ref, send_sems, recv_sems, local_sem):
        x, y, c = lax.axis_index("x"), lax.axis_index("y"), lax.axis_index("c")
        me, sibling = (x, y, c), (x, y, 1 - c)
        chips = [(1 - x, y), (x, 1 - y), (1 - x, 1 - y)]

        def slot(px, py, pc):
            return o_ref.at[4 * px + 2 * py + pc]

        def copy(k, block, to, src=None):
            return _remote(slot(*block) if src is None else src, slot(*block), send_sems.at[k], recv_sems.at[k], to)

        mine = pltpu.make_async_copy(x_ref, slot(*me), local_sem.at[0])
        mine.start()
        first = [copy(0, me, sibling, src=x_ref)]
        first += [copy(1 + j, me, (*chip, c), src=x_ref) for j, chip in enumerate(chips)]
        for cp in first:
            cp.start()
        passed = [copy(4 + j, (*chip, c), sibling) for j, chip in enumerate(chips)]
        for j, chip in enumerate(chips):
            copy(1 + j, (*chip, c), me).wait_recv()
            passed[j].start()
        copy(0, sibling, me).wait_recv()
        for j, chip in enumerate(chips):
            copy(4 + j, (*chip, 1 - c), me).wait_recv()
        for cp in first + passed:
            cp.wait_send()
        mine.wait()

    return pl.pallas_call(
        body, name=name, out_shape=jax.ShapeDtypeStruct((N_DEV,) + x.shape, x.dtype),
        in_specs=[_HBM], out_specs=_HBM, scratch_shapes=_dma_sems(N_DEV - 1),
    )(x)


def sibling_exchange(g8, name):
    def body(x_ref, o_ref, send_sems, recv_sems, _):
        x, y, c = lax.axis_index("x"), lax.axis_index("y"), lax.axis_index("c")
        copies = [_remote(x_ref.at[2 * k + 1 - c], o_ref.at[k], send_sems.at[k], recv_sems.at[k], (x, y, 1 - c))
                  for k in range(N_CHIPS)]
        for cp in copies:
            cp.start()
        for cp in copies:
            cp.wait_recv()
        for cp in copies:
            cp.wait_send()

    return pl.pallas_call(
        body, name=name, out_shape=jax.ShapeDtypeStruct((N_CHIPS,) + g8.shape[1:], g8.dtype),
        in_specs=[_HBM], out_specs=_HBM, scratch_shapes=_dma_sems(N_CHIPS),
    )(g8)


def pair_add(g8, half, name):
    lead = g8.shape[1:]
    C = lead[-1]
    M = 1
    for d in lead[:-1]:
        M *= d
    TR = _row_tile(M, max(16, ADAM_BLOCK // C), 16)

    def body(c_ref, g_ref, h_ref, o_ref):
        o_ref[...] = (g_ref[...].astype(F32) + h_ref[...].astype(F32)).astype(BF16)

    blk = (1, TR, C)
    grid_spec = pltpu.PrefetchScalarGridSpec(
        num_scalar_prefetch=1, grid=(N_CHIPS, M // TR),
        in_specs=[pl.BlockSpec(blk, lambda k, i, c_ref: (2 * k + c_ref[0], i, 0)), pl.BlockSpec(blk, lambda k, i, c_ref: (k, i, 0))],
        out_specs=pl.BlockSpec(blk, lambda k, i, c_ref: (k, i, 0)))
    out = pl.pallas_call(
        body, name=name, grid_spec=grid_spec, out_shape=jax.ShapeDtypeStruct((N_CHIPS, M, C), BF16),
        compiler_params=_cp("parallel", "parallel"),
    )(lax.axis_index("c").astype(jnp.int32).reshape(1), g8.reshape(N_DEV, M, C), half.reshape(N_CHIPS, M, C))
    return out.reshape((N_CHIPS,) + lead)


def chip_exchange(t, name):
    def body(x_ref, o_ref, send_sems, recv_sems, local_sem):
        x, y, c = lax.axis_index("x"), lax.axis_index("y"), lax.axis_index("c")
        mychip = 2 * x + y
        chips = [(1 - x, y), (x, 1 - y), (1 - x, 1 - y)]
        mine = pltpu.make_async_copy(x_ref.at[mychip], o_ref.at[mychip], local_sem.at[0])
        mine.start()
        sends = []
        for j, (px, py) in enumerate(chips):
            cp = _remote(x_ref.at[2 * px + py], o_ref.at[mychip], send_sems.at[j], recv_sems.at[j], (px, py, c))
            cp.start()
            sends.append(cp)
        for j, (px, py) in enumerate(chips):
            _remote(x_ref.at[2 * px + py], o_ref.at[2 * px + py], send_sems.at[j], recv_sems.at[j], (px, py, c)).wait_recv()
        for cp in sends:
            cp.wait_send()
        mine.wait()

    return pl.pallas_call(
        body, name=name, out_shape=jax.ShapeDtypeStruct(t.shape, t.dtype),
        in_specs=[_HBM], out_specs=_HBM, scratch_shapes=_dma_sems(N_CHIPS - 1),
    )(t)


def reduce_exchange(g8, name):
    half = sibling_exchange(g8, "d2d_" + name)
    return chip_exchange(pair_add(g8, half, "pair_" + name), "ici_" + name)


_CARRY_COPIES = {"spread": 4, "chip": 3}


def _carry_shape(kind, x):
    return jax.ShapeDtypeStruct(((N_DEV,) + x.shape) if kind == "spread" else x.shape, x.dtype)


def _carry_scratch(kind, n):
    return [pltpu.SemaphoreType.DMA((n * _CARRY_COPIES[kind],)), pltpu.SemaphoreType.DMA((n * _CARRY_COPIES[kind],)),
            pltpu.SemaphoreType.DMA((n,))]


def _carry_copies(kind, x_refs, o_refs, send_sems, recv_sems, local_sems):
    x, y, c = lax.axis_index("x"), lax.axis_index("y"), lax.axis_index("c")
    chips = [(1 - x, y), (x, 1 - y), (1 - x, 1 - y)]
    sends, recvs, local = [], [], []
    for a, (x_ref, o_ref) in enumerate(zip(x_refs, o_refs)):
        base = a * _CARRY_COPIES[kind]
        if kind == "spread":
            me = 4 * x + 2 * y + c
            peers = [(x, y, 1 - c)] + [(px, py, c) for px, py in chips]
            src = [x_ref] * 4
            mine = [me] * 4
            theirs = [4 * px + 2 * py + pc for px, py, pc in peers]
            local.append(pltpu.make_async_copy(x_ref, o_ref.at[me], local_sems.at[a]))
        else:
            mychip = 2 * x + y
            peers = [(px, py, c) for px, py in chips]
            theirs = [2 * px + py for px, py in chips]
            src = [x_ref.at[t] for t in theirs]
            mine = [mychip] * 3
            local.append(pltpu.make_async_copy(x_ref.at[mychip], o_ref.at[mychip], local_sems.at[a]))
        for k, peer in enumerate(peers):
            sends.append(_remote(src[k], o_ref.at[mine[k]], send_sems.at[base + k], recv_sems.at[base + k], peer))
            recvs.append(_remote(src[k], o_ref.at[theirs[k]], send_sems.at[base + k], recv_sems.at[base + k], peer))
    return sends, recvs, local


def _carry_run(kind, x_refs, o_refs, sems, first, last):
    sends, recvs, local = _carry_copies(kind, x_refs, o_refs, *sems)

    @pl.when(first)
    def _():
        for cp in local + sends:
            cp.start()

    @pl.when(last)
    def _():
        for cp in recvs:
            cp.wait_recv()
        for cp in sends:
            cp.wait_send()
        for cp in local:
            cp.wait()


def gather_pass(o8, name):
    def body(i_ref, o_ref, send_sems, recv_sems, _):
        del i_ref
        x, y, c = lax.axis_index("x"), lax.axis_index("y"), lax.axis_index("c")
        chips = [(1 - x, y), (x, 1 - y), (1 - x, 1 - y)]
        sends = [_remote(o_ref.at[4 * px + 2 * py + c], o_ref.at[4 * px + 2 * py + c], send_sems.at[j], recv_sems.at[j],
                         (x, y, 1 - c)) for j, (px, py) in enumerate(chips)]
        for cp in sends:
            cp.start()
        for j, (px, py) in enumerate(chips):
            slot = o_ref.at[4 * px + 2 * py + 1 - c]
            _remote(slot, slot, send_sems.at[j], recv_sems.at[j], (x, y, 1 - c)).wait_recv()
        for cp in sends:
            cp.wait_send()

    return pl.pallas_call(
        body, name=name, out_shape=jax.ShapeDtypeStruct(o8.shape, o8.dtype), in_specs=[_HBM], out_specs=_HBM,
        scratch_shapes=_dma_sems(N_CHIPS - 1), input_output_aliases={0: 0},
    )(o8)


def mm(a, b, *, mode, grid, a_blk, a_map, b_blk, b_map, o_shape, o_blk, o_map, o_dtype, name, carry=None):
    nk = grid[2]
    na, nb, no = len(a_blk) - 2, len(b_blk) - 2, len(o_blk) - 2
    dims = _DIMS[mode]
    kind, carried = carry if carry is not None else (None, [])
    nc = len(carried)

    def body(*refs):
        a_ref, b_ref = refs[:2]
        o_ref = refs[2 + nc]
        scratch = refs[3 + 2 * nc:]
        if nc:
            pos = [pl.program_id(d) for d in range(3)]
            first = (pos[0] == 0) & (pos[1] == 0) & (pos[2] == 0)
            last = (pos[0] == grid[0] - 1) & (pos[1] == grid[1] - 1) & (pos[2] == grid[2] - 1)
            _carry_run(kind, refs[2:2 + nc], refs[3 + nc:3 + 2 * nc], scratch[-3:], first, last)
        p = _dot(_ld(a_ref, na), _ld(b_ref, nb), dims)
        if nk == 1:
            _st(o_ref, no, p.astype(o_dtype))
        else:
            acc = scratch[0]
            k = pl.program_id(2)

            @pl.when(k == 0)
            def _():
                acc[...] = p

            @pl.when(k > 0)
            def _():
                acc[...] += p

            @pl.when(k == nk - 1)
            def _():
                _st(o_ref, no, acc[...].astype(o_dtype))

    scratch = [pltpu.VMEM(tuple(o_blk[-2:]), F32)] if nk > 1 else []
    out = pl.pallas_call(
        body, name=name, grid=grid,
        out_shape=[jax.ShapeDtypeStruct(o_shape, o_dtype)] + [_carry_shape(kind, x) for x in carried],
        in_specs=[pl.BlockSpec(a_blk, a_map), pl.BlockSpec(b_blk, b_map)] + [_HBM] * nc,
        out_specs=[pl.BlockSpec(o_blk, o_map)] + [_HBM] * nc,
        scratch_shapes=scratch + (_carry_scratch(kind, nc) if nc else []),
        compiler_params=_cp(*((("arbitrary",) * 3) if nc else ("parallel", "parallel", "arbitrary"))),
    )(a, b, *carried)
    return out if nc else out[0]


def proj_fwd(u, w8, o_dtype, name):
    S, D = u.shape
    nc = w8.shape[-1]
    T = min(MM_TILE, S)
    return mm(u, w8, mode="nn", grid=(S // T, N_DEV, 1), a_blk=(T, D), a_map=lambda i, j, k: (i, 0),
              b_blk=(1, 1, D, nc), b_map=lambda i, j, k: (j, 0, 0, 0), o_shape=(S, N_DEV * nc), o_blk=(T, nc),
              o_map=lambda i, j, k: (i, j), o_dtype=o_dtype, name=name)


def proj_bwd_x(dy, w8, name):
    S = dy.shape[0]
    D, nc = w8.shape[-2:]
    T = min(MM_TILE, S)
    return mm(dy, w8, mode="nt", grid=(S // T, 1, N_DEV), a_blk=(T, nc), a_map=lambda i, j, k: (i, k),
              b_blk=(1, 1, D, nc), b_map=lambda i, j, k: (k, 0, 0, 0), o_shape=(S, D), o_blk=(T, D),
              o_map=lambda i, j, k: (i, 0), o_dtype=F32, name=name)


def proj_bwd_w(u, dy, name):
    S, D = u.shape
    nc = dy.shape[1] // N_DEV
    T = min(MM_TILE, S)
    return mm(u, dy, mode="tn", grid=(N_DEV, 1, S // T), a_blk=(T, D), a_map=lambda j, _, k: (k, 0),
              b_blk=(T, nc), b_map=lambda j, _, k: (k, j), o_shape=(N_DEV, 1, D, nc), o_blk=(1, 1, D, nc),
              o_map=lambda j, _, k: (j, 0, 0, 0), o_dtype=BF16, name=name)


def dense_fwd(a, w, name):
    S, D = a.shape
    T = min(MM_TILE, S)
    return mm(a, w, mode="nn", grid=(S // T, 1, 1), a_blk=(T, D), a_map=lambda i, j, k: (i, 0), b_blk=(D, D),
              b_map=lambda i, j, k: (0, 0), o_shape=(S, D), o_blk=(T, D), o_map=lambda i, j, k: (i, 0),
              o_dtype=BF16, name=name)


def dense_bwd_x(dy, w, name):
    S, D = dy.shape
    T = min(MM_TILE, S)
    return mm(dy, w, mode="nt", grid=(S // T, 1, 1), a_blk=(T, D), a_map=lambda i, j, k: (i, 0), b_blk=(D, D),
              b_map=lambda i, j, k: (0, 0), o_shape=(S, D), o_blk=(T, D), o_map=lambda i, j, k: (i, 0),
              o_dtype=BF16, name=name)


def dense_bwd_w(a, dy, name):
    S, D = a.shape
    T = min(MM_TILE, S)
    half = D // 2
    return mm(a, dy, mode="tn", grid=(1, 2, S // T), a_blk=(T, D), a_map=lambda i, j, k: (k, 0), b_blk=(T, half),
              b_map=lambda i, j, k: (k, j), o_shape=(D, D), o_blk=(D, half), o_map=lambda i, j, k: (0, j),
              o_dtype=BF16, name=name)


def _mod_cols(nloc):
    return 256 if nloc % 256 == 0 else nloc


def mod_project(c_pad, w_mod, name):
    L, D, nloc = w_mod.shape
    R = c_pad.shape[0]
    tn = _mod_cols(nloc)

    def body(c_ref, w_ref, o_ref):
        c = c_ref[...]
        s = (c / (1.0 + jnp.exp(-c))).astype(BF16)
        o_ref[0] = _dot(s, w_ref[0].astype(BF16))

    return pl.pallas_call(
        body, name=name, grid=(L, nloc // tn), out_shape=jax.ShapeDtypeStruct((L, R, nloc), F32),
        in_specs=[pl.BlockSpec((R, D), lambda l, j: (0, 0)), pl.BlockSpec((1, D, tn), lambda l, j: (l, 0, j))],
        out_specs=pl.BlockSpec((1, R, tn), lambda l, j: (l, 0, j)), compiler_params=_cp("parallel", "parallel"),
    )(c_pad, w_mod)


def _adam(w, g, m, v):
    m = ADAM_B1 * m + (1.0 - ADAM_B1) * g
    v = ADAM_B2 * v + (1.0 - ADAM_B2) * (g * g)
    m_hat = m / (1.0 - ADAM_B1 ** ADAM_STEP)
    v_hat = v / (1.0 - ADAM_B2 ** ADAM_STEP)
    delta = -ADAM_LR * (m_hat / (jnp.sqrt(v_hat) + ADAM_EPS) + ADAM_WD * w)
    return delta, m, v


def mod_grad_adam(c_pad, dmod_pad, w, m, v, name):
    L, D, nloc = w.shape
    R = c_pad.shape[0]
    tn = _mod_cols(nloc)

    def body(c_ref, d_ref, w_ref, m_ref, v_ref, g_ref, dl_ref, nm_ref, nv_ref):
        c = c_ref[...]
        s = (c / (1.0 + jnp.exp(-c))).astype(BF16)
        g = _dot(s, d_ref[0].astype(BF16), _TN)
        delta, nm, nv = _adam(w_ref[0], g, m_ref[0], v_ref[0])
        g_ref[0], dl_ref[0], nm_ref[0], nv_ref[0] = g, delta, nm, nv

    wspec = pl.BlockSpec((1, D, tn), lambda l, j: (l, 0, j))
    out = jax.ShapeDtypeStruct(w.shape, F32)
    return pl.pallas_call(
        body, name=name, grid=(L, nloc // tn), out_shape=(out,) * 4,
        in_specs=[pl.BlockSpec((R, D), lambda l, j: (0, 0)), pl.BlockSpec((1, R, tn), lambda l, j: (l, 0, j)),
                  wspec, wspec, wspec],
        out_specs=(wspec,) * 4, compiler_params=_cp("parallel", "parallel"),
    )(c_pad, dmod_pad, w, m, v)


def resid_norm(h_prev, y_prev, gate_prev, gn, sc, sh, u_dtype, name):
    S, D = h_prev.shape
    T = min(ROW_TILE, S)
    has_res = y_prev is not None

    def body(*refs):
        if has_res:
            h_ref, y_ref, gate_ref, gn_ref, sc_ref, sh_ref, ho_ref, u_ref = refs
            h = h_ref[...] + gate_ref[...] * y_ref[...].astype(F32)
            ho_ref[...] = h
        else:
            h_ref, gn_ref, sc_ref, sh_ref, u_ref = refs
            h = h_ref[...]
        r = lax.rsqrt(jnp.mean(h * h, axis=-1, keepdims=True) + EPS)
        u = (h * r) * gn_ref[...] * (1.0 + sc_ref[...]) + sh_ref[...]
        u_ref[...] = u.astype(u_dtype)

    row = pl.BlockSpec((T, D), lambda i: (i, 0))
    vec = _vec_spec(D, 1)
    if has_res:
        h, u = pl.pallas_call(
            body, name=name, grid=(S // T,), out_shape=(jax.ShapeDtypeStruct((S, D), F32), jax.ShapeDtypeStruct((S, D), u_dtype)),
            in_specs=[row, row, vec, vec, vec, vec], out_specs=(row, row), compiler_params=_cp("parallel"),
        )(h_prev, y_prev, gate_prev, gn, sc, sh)
        return h, u
    u = pl.pallas_call(
        body, name=name, grid=(S // T,), out_shape=jax.ShapeDtypeStruct((S, D), u_dtype),
        in_specs=[row, vec, vec, vec], out_specs=row, compiler_params=_cp("parallel"),
    )(h_prev, gn, sc, sh)
    return h_prev, u


def gate_bwd(dh, y, gate, name):
    S, D = dh.shape
    T = min(ROW_TILE, S)

    def body(dh_ref, y_ref, gate_ref, dy_ref, dg_ref):
        i = pl.program_id(0)
        d = dh_ref[...]
        dy_ref[...] = (d * gate_ref[...]).astype(BF16)
        part = jnp.sum(d * y_ref[...].astype(F32), axis=0, keepdims=True)

        @pl.when(i == 0)
        def _():
            dg_ref[...] = part

        @pl.when(i > 0)
        def _():
            dg_ref[...] += part

    row = pl.BlockSpec((T, D), lambda i: (i, 0))
    vec = _vec_spec(D, 1)
    return pl.pallas_call(
        body, name=name, grid=(S // T,), out_shape=(jax.ShapeDtypeStruct((S, D), BF16), jax.ShapeDtypeStruct((1, D), F32)),
        in_specs=[row, row, vec], out_specs=(row, vec), compiler_params=_cp("arbitrary"),
    )(dh, y, gate)


def norm_bwd(du, h, dh_res, gn, sc, name):
    S, D = h.shape
    T = min(ROW_TILE, S)

    def body(du_ref, h_ref, dr_ref, gn_ref, sc_ref, dh_ref, dgn_ref, dsc_ref, dsh_ref):
        i = pl.program_id(0)
        d = du_ref[...].astype(F32)
        hh = h_ref[...]
        r = lax.rsqrt(jnp.mean(hh * hh, axis=-1, keepdims=True) + EPS)
        n = hh * r
        gn_v = gn_ref[...]
        dng = d * (1.0 + sc_ref[...])
        dn = dng * gn_v
        dh_ref[...] = dr_ref[...] + r * (dn - n * jnp.mean(dn * n, axis=-1, keepdims=True))
        parts = (jnp.sum(dng * n, axis=0, keepdims=True), jnp.sum(d * (n * gn_v), axis=0, keepdims=True),
                 jnp.sum(d, axis=0, keepdims=True))

        @pl.when(i == 0)
        def _():
            for ref, part in zip((dgn_ref, dsc_ref, dsh_ref), parts):
                ref[...] = part

        @pl.when(i > 0)
        def _():
            for ref, part in zip((dgn_ref, dsc_ref, dsh_ref), parts):
                ref[...] += part

    row = pl.BlockSpec((T, D), lambda i: (i, 0))
    vec = _vec_spec(D, 1)
    vshape = jax.ShapeDtypeStruct((1, D), F32)
    return pl.pallas_call(
        body, name=name, grid=(S // T,), out_shape=(jax.ShapeDtypeStruct((S, D), F32), vshape, vshape, vshape),
        in_specs=[row, row, row, vec, vec], out_specs=(row, vec, vec, vec), compiler_params=_cp("arbitrary"),
    )(du, h, dh_res, gn, sc)


def final_loss(h_prev, y_prev, gate_prev, fg, target, name):
    S, D = h_prev.shape
    T = min(ROW_TILE, S)

    def body(h_ref, y_ref, gate_ref, fg_ref, t_ref, dh_ref, dfg_ref, loss_ref):
        i = pl.program_id(0)
        h = h_ref[...] + gate_ref[...] * y_ref[...].astype(F32)
        r = lax.rsqrt(jnp.mean(h * h, axis=-1, keepdims=True) + EPS)
        n = h * r
        g = fg_ref[...]
        err = n * g - t_ref[...]
        dout = err * (1.0 / D)
        dn = dout * g
        dh_ref[...] = r * (dn - n * jnp.mean(dn * n, axis=-1, keepdims=True))
        dfg = jnp.sum(dout * n, axis=0, keepdims=True)
        part = 0.5 * jnp.sum(jnp.mean(err * err, axis=-1, keepdims=True), axis=0, keepdims=True)
        part = jnp.broadcast_to(part, (1, LANES))

        @pl.when(i == 0)
        def _():
            dfg_ref[...] = dfg
            loss_ref[...] = part

        @pl.when(i > 0)
        def _():
            dfg_ref[...] += dfg
            loss_ref[...] += part

    row = pl.BlockSpec((T, D), lambda i: (i, 0))
    vec = _vec_spec(D, 1)
    return pl.pallas_call(
        body, name=name, grid=(S // T,),
        out_shape=(jax.ShapeDtypeStruct((S, D), F32), jax.ShapeDtypeStruct((1, D), F32), jax.ShapeDtypeStruct((1, LANES), F32)),
        in_specs=[row, row, vec, vec, row], out_specs=(row, vec, pl.BlockSpec((1, LANES), lambda i: (0, 0))),
        compiler_params=_cp("arbitrary"),
    )(h_prev, y_prev, gate_prev, fg, target)


def _prev_halo(T):
    return lambda i: (jnp.maximum(i * (T // HALO) - 1, 0), 0)


def _next_halo(T, S):
    return lambda i: (jnp.minimum((i + 1) * (T // HALO), S // HALO - 1), 0)


def pool_fwd(u, w, ps, name):
    S, D = u.shape
    T = min(ROW_TILE, S)
    C = D // len(POOL_WINDOWS)

    def body(uc_ref, up_ref, w_ref, ps_ref, diff_ref, y_ref):
        i = pl.program_id(0)
        t = lax.broadcasted_iota(jnp.int32, (T, 1), 0) + i * T
        for g, win in enumerate(POOL_WINDOWS):
            cols = slice(g * C, (g + 1) * C)
            cur = uc_ref[:, cols]
            prev = jnp.where(i > 0, up_ref[:, cols], 0.0)
            s = jnp.concatenate([prev, cur], axis=0)
            k = 1
            while k < win:
                s = s + pltpu.roll(s, k, 0)
                k *= 2
            cnt = jnp.minimum(t + 1, win).astype(F32)
            diff = (s[HALO:, :] / cnt - cur).astype(BF16)
            diff_ref[:, cols] = diff
            y_ref[:, cols] = (_dot(diff, w_ref[g]) * ps_ref[:, cols]).astype(BF16)

    row = pl.BlockSpec((T, D), lambda i: (i, 0))
    out = jax.ShapeDtypeStruct((S, D), BF16)
    return pl.pallas_call(
        body, name=name, grid=(S // T,), out_shape=(out, out),
        in_specs=[row, pl.BlockSpec((HALO, D), _prev_halo(T)), pl.BlockSpec(w.shape, lambda i: (0, 0, 0)), _vec_spec(D, 1)],
        out_specs=(row, row), compiler_params=_cp("parallel"),
    )(u, u, w, ps)


def pool_bwd(dy, diff, w, ps, name):
    S, D = dy.shape
    T = min(ROW_TILE, S)
    G = len(POOL_WINDOWS)
    C = D // G

    def body(dy_ref, diff_ref, w_ref, ps_ref, dd_ref, dw_ref, dps_ref):
        i = pl.program_id(0)
        for g in range(G):
            cols = slice(g * C, (g + 1) * C)
            diff = diff_ref[:, cols]
            d = dy_ref[:, cols].astype(F32)
            ypre = _dot(diff, w_ref[g])
            dps = jnp.sum(d * ypre, axis=0, keepdims=True)
            dyp = (d * ps_ref[:, cols]).astype(BF16)
            dd_ref[:, cols] = _dot(dyp, w_ref[g], _NT)
            dw = _dot(diff, dyp, _TN)

            @pl.when(i == 0)
            def _():
                dw_ref[g] = dw
                dps_ref[:, cols] = dps

            @pl.when(i > 0)
            def _():
                dw_ref[g] += dw
                dps_ref[:, cols] += dps

    row = pl.BlockSpec((T, D), lambda i: (i, 0))
    wspec = pl.BlockSpec(w.shape, lambda i: (0, 0, 0))
    return pl.pallas_call(
        body, name=name, grid=(S // T,),
        out_shape=(jax.ShapeDtypeStruct((S, D), F32), jax.ShapeDtypeStruct(w.shape, F32), jax.ShapeDtypeStruct((1, D), F32)),
        in_specs=[row, row, wspec, _vec_spec(D, 1)], out_specs=(row, wspec, _vec_spec(D, 1)),
        compiler_params=_cp("arbitrary"),
    )(dy, diff, w, ps)


def pool_window_bwd(dd, name):
    S, D = dd.shape
    T = min(ROW_TILE, S)
    C = D // len(POOL_WINDOWS)
    n = T + HALO
    last = S // T - 1

    def body(dc_ref, dn_ref, du_ref):
        i = pl.program_id(0)
        t = lax.broadcasted_iota(jnp.int32, (n, 1), 0) + i * T
        for g, win in enumerate(POOL_WINDOWS):
            cols = slice(g * C, (g + 1) * C)
            cur = dc_ref[:, cols]
            nxt = jnp.where(i < last, dn_ref[:, cols], 0.0)
            cnt = jnp.minimum(t + 1, win).astype(F32)
            s = jnp.concatenate([cur, nxt], axis=0) / cnt
            k = 1
            while k < win:
                s = s + pltpu.roll(s, n - k, 0)
                k *= 2
            du_ref[:, cols] = s[:T, :] - cur

    row = pl.BlockSpec((T, D), lambda i: (i, 0))
    return pl.pallas_call(
        body, name=name, grid=(S // T,), out_shape=jax.ShapeDtypeStruct((S, D), F32),
        in_specs=[row, pl.BlockSpec((HALO, D), _next_halo(T, S))], out_specs=row, compiler_params=_cp("parallel"),
    )(dd, dd)


def _conv_chunk(D):
    return 512 if D % 512 == 0 else D


def conv_fwd(U, cw, name):
    S = U.shape[0]
    D = U.shape[1] // 3
    T = min(CONV_TILE, S)
    CH = _conv_chunk(D)

    def body(uc_ref, up_ref, cw_ref, q_ref):
        i = pl.program_id(0)
        for j in range(D // CH):
            cols = slice(j * CH, (j + 1) * CH)
            ccols = slice(D + j * CH, D + (j + 1) * CH)
            vcols = slice(2 * D + j * CH, 2 * D + (j + 1) * CH)
            zp = jnp.where(i > 0, up_ref[:, ccols] * up_ref[:, vcols], 0.0)
            z = jnp.concatenate([zp, uc_ref[:, ccols] * uc_ref[:, vcols]], axis=0)
            zc = cw_ref[2, :, cols] * z + cw_ref[1, :, cols] * pltpu.roll(z, 1, 0) + cw_ref[0, :, cols] * pltpu.roll(z, 2, 0)
            q_ref[:, cols] = (uc_ref[:, cols] * zc[HALO:, :]).astype(BF16)

    return pl.pallas_call(
        body, name=name, grid=(S // T,), out_shape=jax.ShapeDtypeStruct((S, D), BF16),
        in_specs=[pl.BlockSpec((T, 3 * D), lambda i: (i, 0)), pl.BlockSpec((HALO, 3 * D), _prev_halo(T)),
                  pl.BlockSpec((3, 1, D), lambda i: (0, 0, 0))],
        out_specs=pl.BlockSpec((T, D), lambda i: (i, 0)), compiler_params=_cp("parallel"),
    )(U, U, cw)


def conv_bwd(dq, U, cw, name):
    S = U.shape[0]
    D = U.shape[1] // 3
    T = min(CONV_TILE, S)
    CH = _conv_chunk(D)
    n = T + HALO
    last = S // T - 1

    def body(dq_ref, dqn_ref, uc_ref, up_ref, un_ref, cw_ref, du_ref, dcw_ref):
        i = pl.program_id(0)
        for j in range(D // CH):
            cols = slice(j * CH, (j + 1) * CH)
            ccols = slice(D + j * CH, D + (j + 1) * CH)
            vcols = slice(2 * D + j * CH, 2 * D + (j + 1) * CH)
            w0, w1, w2 = cw_ref[0, :, cols], cw_ref[1, :, cols], cw_ref[2, :, cols]
            b, c, v = uc_ref[:, cols], uc_ref[:, ccols], uc_ref[:, vcols]
            dqc = dq_ref[:, cols].astype(F32)
            zp = jnp.where(i > 0, up_ref[:, ccols] * up_ref[:, vcols], 0.0)
            z = jnp.concatenate([zp, c * v], axis=0)
            z1 = pltpu.roll(z, 1, 0)[HALO:, :]
            z2 = pltpu.roll(z, 2, 0)[HALO:, :]
            z0 = z[HALO:, :]
            zc = w2 * z0 + w1 * z1 + w0 * z2
            dzc = dqc * b
            dzn = jnp.where(i < last, dqn_ref[:, cols].astype(F32) * un_ref[:, cols], 0.0)
            e = jnp.concatenate([dzc, dzn], axis=0)
            dz = (w2 * e + w1 * pltpu.roll(e, n - 1, 0) + w0 * pltpu.roll(e, n - 2, 0))[:T, :]
            du_ref[:, cols] = (dqc * zc).astype(BF16)
            du_ref[:, ccols] = (dz * v).astype(BF16)
            du_ref[:, vcols] = (dz * c).astype(BF16)
            parts = [jnp.sum(dzc * zz, axis=0, keepdims=True) for zz in (z2, z1, z0)]

            @pl.when(i == 0)
            def _():
                for k in range(3):
                    dcw_ref[k, :, cols] = parts[k]

            @pl.when(i > 0)
            def _():
                for k in range(3):
                    dcw_ref[k, :, cols] += parts[k]

    return pl.pallas_call(
        body, name=name, grid=(S // T,),
        out_shape=(jax.ShapeDtypeStruct((S, 3 * D), BF16), jax.ShapeDtypeStruct((3, 1, D), F32)),
        in_specs=[pl.BlockSpec((T, D), lambda i: (i, 0)), pl.BlockSpec((HALO, D), _next_halo(T, S)),
                  pl.BlockSpec((T, 3 * D), lambda i: (i, 0)), pl.BlockSpec((HALO, 3 * D), _prev_halo(T)),
                  pl.BlockSpec((HALO, 3 * D), _next_halo(T, S)), pl.BlockSpec((3, 1, D), lambda i: (0, 0, 0))],
        out_specs=(pl.BlockSpec((T, 3 * D), lambda i: (i, 0)), pl.BlockSpec((3, 1, D), lambda i: (0, 0, 0))),
        compiler_params=_cp("arbitrary"),
    )(dq, dq, U, U, U, cw)


MASKED_LOG = -1e30


def _sb_logits(z, q0, k0, masked):
    z = z * (HEAD_DIM ** -0.5)
    lb = jnp.minimum(z, 0.0) - jnp.log(1.0 + jnp.exp(-jnp.abs(z)))
    l1 = lb - z
    if masked:
        causal = (lax.broadcasted_iota(jnp.int32, z.shape, 1) + k0) < (lax.broadcasted_iota(jnp.int32, z.shape, 0) + q0)
        lb = jnp.where(causal, lb, MASKED_LOG)
        l1 = jnp.where(causal, l1, 0.0)
    return lb, l1


def _key_sums(l1, tri):
    hi = l1.astype(BF16)
    lo = (l1 - hi.astype(F32)).astype(BF16)
    return _dot(hi, tri) + _dot(lo, tri)


def _tri(B, cmp):
    row = lax.broadcasted_iota(jnp.int32, (B, B), 0)
    col = lax.broadcasted_iota(jnp.int32, (B, B), 1)
    return cmp(row, col).astype(BF16)


def attn_fwd(qkv, name):
    S = qkv.shape[0]
    D = qkv.shape[1] // 3
    H = D // HEAD_DIM
    BK = min(ATT_K, S)
    BQ = min(ATT_Q, S)
    ND = BQ // BK

    def body(q_ref, k_ref, v_ref, o_ref, lt_ref):
        later = _tri(BK, lambda j, s: j > s)

        def q_loop(qi, _):
            q0 = pl.multiple_of(qi * BQ, BQ)
            qb = q_ref[pl.ds(q0, BQ), :]
            n_off = qi * ND

            def scores(kb):
                k0 = pl.multiple_of(kb * BK, BK)
                return _dot(qb, k_ref[pl.ds(k0, BK), :], _NT)

            def accumulate(kb, z, masked, carry, oacc):
                k0 = pl.multiple_of(kb * BK, BK)
                lb, l1 = _sb_logits(z, q0, k0, masked)
                suffix = _key_sums(l1, later) + carry
                a = jnp.exp(lb + suffix)
                oacc = oacc + _dot(a.astype(BF16), v_ref[pl.ds(k0, BK), :])
                return carry + jnp.sum(l1, axis=1, keepdims=True), oacc

            carry, oacc = jnp.zeros((BQ, 1), F32), jnp.zeros((BQ, HEAD_DIM), F32)
            z = scores(n_off + ND - 1)
            for d in reversed(range(ND)):
                z_next = scores(jnp.maximum(n_off + d - 1, 0))
                carry, oacc = accumulate(n_off + d, z, True, carry, oacc)
                z = z_next

            def k_loop(t, state):
                carry, oacc, z = state
                kb = n_off - 1 - t
                z_next = scores(jnp.maximum(kb - 1, 0))
                carry, oacc = accumulate(kb, z, False, carry, oacc)
                return carry, oacc, z_next

            total, oacc, _ = lax.fori_loop(0, n_off, k_loop, (carry, oacc, z))
            o_ref[pl.ds(q0, BQ), :] = oacc.astype(BF16)
            lt_ref[pl.ds(q0, BQ), :] = jnp.broadcast_to(total, (BQ, HEAD_DIM))
            return 0

        lax.fori_loop(0, S // BQ, q_loop, 0)

    def head(off):
        return pl.BlockSpec((S, HEAD_DIM), lambda h: (0, off + h))

    return pl.pallas_call(
        body, name=name, grid=(H,), out_shape=(jax.ShapeDtypeStruct((S, D), BF16), jax.ShapeDtypeStruct((S, D), F32)),
        in_specs=[head(0), head(H), head(2 * H)], out_specs=(head(0), head(0)), compiler_params=_cp("parallel"),
    )(qkv, qkv, qkv)


def attn_bwd(qkv, lt, do, name):
    S = qkv.shape[0]
    D = qkv.shape[1] // 3
    H = D // HEAD_DIM
    BK = min(ATT_K, S)
    BQ = min(ATT_Q_BWD, S)
    ND = BQ // BK
    scale = HEAD_DIM ** -0.5

    def body(q_ref, k_ref, v_ref, lt_ref, do_ref, dq_ref, dk_ref, dv_ref, dk_acc, dv_acc):
        upto = _tri(BK, lambda j, s: j <= s)
        before = _tri(BK, lambda j, s: j < s)
        dk_acc[...] = jnp.zeros_like(dk_acc)
        dv_acc[...] = jnp.zeros_like(dv_acc)

        def q_loop(qi, _):
            q0 = pl.multiple_of(qi * BQ, BQ)
            qb = q_ref[pl.ds(q0, BQ), :]
            dob = do_ref[pl.ds(q0, BQ), :]
            total = lt_ref[pl.ds(q0, BQ), :][:, :1]
            n_off = qi * ND

            def scores(kb):
                k0 = pl.multiple_of(kb * BK, BK)
                return _dot(qb, k_ref[pl.ds(k0, BK), :], _NT), _dot(dob, v_ref[pl.ds(k0, BK), :], _NT)

            def accumulate(kb, z, da, masked, lsum, gsum, dqacc):
                k0 = pl.multiple_of(kb * BK, BK)
                lb, l1 = _sb_logits(z, q0, k0, masked)
                suffix = total - lsum - _key_sums(l1, upto)
                a = jnp.exp(lb + suffix)
                g = a * da
                gpre = gsum + _dot(g.astype(BF16), before)
                beta = jnp.exp(lb)
                dz = ((g * (1.0 - beta) - gpre * beta) * scale).astype(BF16)
                dqacc = dqacc + _dot(dz, k_ref[pl.ds(k0, BK), :])
                dk_acc[pl.ds(k0, BK), :] += _dot(dz, qb, _TN)
                dv_acc[pl.ds(k0, BK), :] += _dot(a.astype(BF16), dob, _TN)
                return (lsum + jnp.sum(l1, axis=1, keepdims=True), gsum + jnp.sum(g, axis=1, keepdims=True), dqacc)

            def k_loop(t, state):
                lsum, gsum, dqacc, z, da = state
                z_next, da_next = scores(t + 1)
                lsum, gsum, dqacc = accumulate(t, z, da, False, lsum, gsum, dqacc)
                return lsum, gsum, dqacc, z_next, da_next

            zero = jnp.zeros((BQ, 1), F32)
            lsum, gsum, dqacc, z, da = lax.fori_loop(
                0, n_off, k_loop, (zero, zero, jnp.zeros((BQ, HEAD_DIM), F32), *scores(0)))
            for d in range(ND):
                if d + 1 < ND:
                    z_next, da_next = scores(n_off + d + 1)
                lsum, gsum, dqacc = accumulate(n_off + d, z, da, True, lsum, gsum, dqacc)
                z, da = z_next, da_next
            dq_ref[pl.ds(q0, BQ), :] = dqacc.astype(BF16)
            return 0

        lax.fori_loop(0, S // BQ, q_loop, 0)
        dk_ref[...] = dk_acc[...].astype(BF16)
        dv_ref[...] = dv_acc[...].astype(BF16)

    def head(off):
        return pl.BlockSpec((S, HEAD_DIM), lambda h: (0, off + h))

    out = jax.ShapeDtypeStruct((S, D), BF16)
    return pl.pallas_call(
        body, name=name, grid=(H,), out_shape=(out, out, out),
        in_specs=[head(0), head(H), head(2 * H), head(0), head(0)], out_specs=(head(0), head(0), head(0)),
        scratch_shapes=[pltpu.VMEM((S, HEAD_DIM), F32), pltpu.VMEM((S, HEAD_DIM), F32)], compiler_params=_cp("parallel"),
    )(qkv, qkv, qkv, lt, do)


def ffn_fwd(u, wgu8, wd8, name, spread=()):
    S, D = u.shape
    n = wgu8.shape[-1]
    T = min(FFN_TILE, S)
    nc = len(spread)
    nsteps = S // T

    def body(*refs):
        u_ref, wgu_ref, wd_ref = refs[:3]
        gp_ref, y_ref = refs[3 + nc:5 + nc]
        acc = refs[5 + 2 * nc]
        i, j = pl.program_id(0), pl.program_id(1)
        if nc:
            _carry_run("spread", refs[3:3 + nc], refs[5 + nc:5 + 2 * nc], refs[6 + 2 * nc:],
                       (i == 0) & (j == 0), (i == nsteps - 1) & (j == N_DEV - 1))
        uu = u_ref[...]
        g = _dot(uu, wgu_ref[0, 0])
        p = _dot(uu, wgu_ref[0, 1])
        gp_ref[0, 0] = g.astype(BF16)
        gp_ref[1, 0] = p.astype(BF16)
        a = (g / (1.0 + jnp.exp(-g)) * p).astype(BF16)
        part = _dot(a, wd_ref[0])

        @pl.when(j == 0)
        def _():
            acc[...] = part

        @pl.when(j > 0)
        def _():
            acc[...] += part

        @pl.when(j == N_DEV - 1)
        def _():
            y_ref[...] = acc[...].astype(BF16)

    return pl.pallas_call(
        body, name=name, grid=(nsteps, N_DEV),
        out_shape=[jax.ShapeDtypeStruct((2, N_DEV, S, n), BF16), jax.ShapeDtypeStruct((S, D), BF16)]
        + [_carry_shape("spread", x) for x in spread],
        in_specs=[pl.BlockSpec((T, D), lambda i, j: (i, 0)), pl.BlockSpec((1, 2, D, n), lambda i, j: (j, 0, 0, 0)),
                  pl.BlockSpec((1, n, D), lambda i, j: (j, 0, 0))] + [_HBM] * nc,
        out_specs=[pl.BlockSpec((2, 1, T, n), lambda i, j: (0, j, i, 0)), pl.BlockSpec((T, D), lambda i, j: (i, 0))]
        + [_HBM] * nc,
        scratch_shapes=[pltpu.VMEM((T, D), F32)] + (_carry_scratch("spread", nc) if nc else []),
        compiler_params=_cp("arbitrary" if nc else "parallel", "arbitrary"),
    )(u, wgu8, wd8, *spread)


def ffn_bwd_act(dy, gp, wd8, name):
    S, D = dy.shape
    n = gp.shape[-1]
    T = min(FFN_TILE, S)

    def body(dy_ref, gp_ref, wd_ref, a_ref, dgp_ref):
        da = _dot(dy_ref[...], wd_ref[0], _NT)
        g = gp_ref[0, 0].astype(F32)
        p = gp_ref[1, 0].astype(F32)
        s = 1.0 / (1.0 + jnp.exp(-g))
        sl = g * s
        a_ref[0] = (sl * p).astype(BF16)
        dgp_ref[0, 0] = (da * p * (s * (1.0 + g * (1.0 - s)))).astype(BF16)
        dgp_ref[1, 0] = (da * sl).astype(BF16)

    gp_spec = pl.BlockSpec((2, 1, T, n), lambda i, j: (0, j, i, 0))
    return pl.pallas_call(
        body, name=name, grid=(S // T, N_DEV),
        out_shape=(jax.ShapeDtypeStruct((N_DEV, S, n), BF16), jax.ShapeDtypeStruct((2, N_DEV, S, n), BF16)),
        in_specs=[pl.BlockSpec((T, D), lambda i, j: (i, 0)), gp_spec, pl.BlockSpec((1, n, D), lambda i, j: (j, 0, 0))],
        out_specs=(pl.BlockSpec((1, T, n), lambda i, j: (j, i, 0)), gp_spec), compiler_params=_cp("parallel", "parallel"),
    )(dy, gp, wd8)


def ffn_bwd_x(dgp, wgu8, name, carry=None):
    S, n = dgp.shape[-2:]
    D = wgu8.shape[-2]
    T = min(MM_TILE, S)
    return mm(dgp, wgu8, mode="nt", grid=(S // T, 1, 2 * N_DEV), a_blk=(1, 1, T, n),
              a_map=lambda i, j, k: (k // N_DEV, k % N_DEV, i, 0), b_blk=(1, 1, D, n),
              b_map=lambda i, j, k: (k % N_DEV, k // N_DEV, 0, 0), o_shape=(S, D), o_blk=(T, D),
              o_map=lambda i, j, k: (i, 0), o_dtype=F32, name=name, carry=carry)


def ffn_bwd_wgu(u, dgp, name):
    S, D = u.shape
    n = dgp.shape[-1]
    T = min(MM_TILE, S)
    return mm(u, dgp, mode="tn", grid=(N_DEV, 2, S // T), a_blk=(T, D), a_map=lambda j, t, k: (k, 0),
              b_blk=(1, 1, T, n), b_map=lambda j, t, k: (t, j, k, 0), o_shape=(N_DEV, 2, D, n), o_blk=(1, 1, D, n),
              o_map=lambda j, t, k: (j, t, 0, 0), o_dtype=BF16, name=name)


def ffn_bwd_wd(a8, dy, name):
    n = a8.shape[-1]
    S, D = dy.shape
    T = min(MM_TILE, S)
    return mm(a8, dy, mode="tn", grid=(N_DEV, 1, S // T), a_blk=(1, T, n), a_map=lambda j, _, k: (j, k, 0),
              b_blk=(T, D), b_map=lambda j, _, k: (k, 0), o_shape=(N_DEV, n, D), o_blk=(1, n, D),
              o_map=lambda j, _, k: (j, 0, 0), o_dtype=BF16, name=name)


def adam_update(recv, a_idx, w, m, v, name, layer=None, into=None):
    L, R, C = w.shape
    TR = _row_tile(R, max(8, ADAM_BLOCK // C))
    nsrc = recv.shape[0]

    def body(r_ref, w_ref, m_ref, v_ref, *rest):
        g_ref, d_ref, nm_ref, nv_ref = rest[-4:]
        g = r_ref[0, 0, 0].astype(F32)
        for k in range(1, nsrc):
            g = g + r_ref[k, 0, 0].astype(F32)
        delta, nm, nv = _adam(w_ref[0], g, m_ref[0], v_ref[0])
        g_ref[0], d_ref[0], nm_ref[0], nv_ref[0] = g, delta, nm, nv

    if layer is None:
        grid, wmap, rmap = (L, R // TR), (lambda l, i: (l, i, 0)), (lambda l, i: (0, a_idx, l, i, 0))
    else:
        grid, wmap, rmap = (1, R // TR), (lambda l, i: (layer, i, 0)), (lambda l, i: (0, a_idx, 0, i, 0))
    wspec = pl.BlockSpec((1, TR, C), wmap)
    out = jax.ShapeDtypeStruct(w.shape, F32)
    prior = list(into) if into is not None else []
    return pl.pallas_call(
        body, name=name, grid=grid, out_shape=(out,) * 4,
        in_specs=[pl.BlockSpec((nsrc, 1, 1, TR, C), rmap), wspec, wspec, wspec] + [_HBM] * len(prior),
        out_specs=(wspec,) * 4, input_output_aliases={4 + k: k for k in range(len(prior))},
        compiler_params=_cp("parallel", "parallel"),
    )(recv, w, m, v, *prior)


def sum_devices(x, name):
    _, R, C = x.shape

    def body(x_ref, o_ref):
        s = x_ref[0]
        for k in range(1, N_DEV):
            s = s + x_ref[k]
        o_ref[...] = s

    return pl.pallas_call(
        body, name=name, out_shape=jax.ShapeDtypeStruct((R, C), F32),
        in_specs=[pl.BlockSpec(memory_space=pltpu.VMEM)], out_specs=pl.BlockSpec(memory_space=pltpu.VMEM),
    )(x)


def _pack(parts):
    flat = jnp.concatenate([p.reshape(-1) for p in parts])
    pad = (-flat.shape[0]) % (8 * LANES)
    return jnp.pad(flat, (0, pad)).reshape(-1, LANES)


def _unpack(packed, shapes, lead=()):
    flat = packed.reshape(lead + (-1,))
    out, off = [], 0
    for s in shapes:
        size = 1
        for d in s:
            size *= d
        out.append(flat[..., off:off + size].reshape(lead + tuple(s)))
        off += size
    return out


def kernel(x, c, norm_mix_g, norm_ffn_g, w_mod, b_mod, pool_w, pool_scale, conv_w_in, conv_w, conv_w_out, sb_w_qkv, sb_w_o, ffn_w_gate, ffn_w_up, ffn_w_down, final_g, loss_target, m_norm_mix_g, m_norm_ffn_g, m_w_mod, m_b_mod, m_pool_w, m_pool_scale, m_conv_w_in, m_conv_w, m_conv_w_out, m_sb_w_qkv, m_sb_w_o, m_ffn_w_gate, m_ffn_w_up, m_ffn_w_down, m_final_g, v_norm_mix_g, v_norm_ffn_g, v_w_mod, v_b_mod, v_pool_w, v_pool_scale, v_conv_w_in, v_conv_w, v_conv_w_out, v_sb_w_qkv, v_sb_w_o, v_ffn_w_gate, v_ffn_w_up, v_ffn_w_down, v_final_g):
    S, D = x.shape[1:]
    L = N_LAYERS
    G = len(POOL_WINDOWS)
    C = D // G
    dloc = D // N_DEV
    nmod = w_mod.shape[-1]
    me = _my_index()

    def ffn_local(l):
        return [jnp.stack([ffn_w_gate[l], ffn_w_up[l]]).astype(BF16), ffn_w_down[l].astype(BF16)]

    wgu8 = {0: all_gather(ffn_local(0)[0], "ag_ffn_gate_up_0")}
    wd8 = {0: all_gather(ffn_local(0)[1], "ag_ffn_down_0")}
    wcin8 = all_gather(conv_w_in.astype(BF16), "ag_conv_in")
    wcout = all_gather(conv_w_out.astype(BF16), "ag_conv_out").reshape(D, D)
    wqkv8 = all_gather(sb_w_qkv.astype(BF16), "ag_sb_qkv")
    wo = all_gather(sb_w_o.astype(BF16), "ag_sb_o").reshape(D, D)
    pw = all_gather(pool_w.astype(BF16), "ag_pool_w")
    pw = pw.transpose(1, 2, 0, 3, 4).reshape(pool_w.shape[0], G, C, C)
    small_shapes = [(1, D), (3, dloc), (pool_scale.shape[0], dloc)]
    small = all_gather(_pack([c, conv_w, pool_scale]), "ag_small_in")
    c_all, cw_all, ps_all = _unpack(small, small_shapes, (N_DEV,))
    c_pad = jnp.pad(c_all.reshape(N_DEV, D), ((0, N_DEV), (0, 0)))
    cw = cw_all.transpose(1, 0, 2).reshape(3, 1, D)
    ps = ps_all.transpose(1, 0, 2).reshape(-1, D)

    mod_part = mod_project(c_pad, w_mod, "mod_project")
    mod_all = all_gather(mod_part, "ag_mod")
    mod = lax.dynamic_index_in_dim(mod_all, me, axis=2, keepdims=False)
    mod = mod.transpose(1, 0, 2).reshape(L, N_MOD * D) + b_mod

    def mod_vec(i, k):
        return mod[i, k * D:(k + 1) * D].reshape(1, D)

    h = x[0]
    y_prev = gate_prev = None
    saved = []
    for i in range(L):
        sh_m, sc_m, g_m, sh_f, sc_f, g_f = (mod_vec(i, k) for k in range(N_MOD))
        kind, j = i % 3, i // 3
        gn_m, gn_f = norm_mix_g[i].reshape(1, D), norm_ffn_g[i].reshape(1, D)
        h_mix, u = resid_norm(h, y_prev, gate_prev, gn_m, sc_m, sh_m, F32 if kind == 0 else BF16, f"norm_mix_{i}")
        if kind == 0:
            diff, y = pool_fwd(u, pw[j], ps[j].reshape(1, D), f"pool_fwd_{i}")
            mix = (diff,)
        elif kind == 1:
            U = proj_fwd(u, wcin8, F32, f"conv_in_{i}")
            q = conv_fwd(U, cw, f"conv_fwd_{i}")
            y = dense_fwd(q, wcout, f"conv_out_{i}")
            mix = (u, U, q)
        else:
            qkv = proj_fwd(u, wqkv8, BF16, f"sb_qkv_{i}")
            o, lt = attn_fwd(qkv, f"attn_fwd_{i}")
            y = dense_fwd(o, wo, f"sb_out_{i}")
            mix = (u, qkv, o, lt)
        h_ffn, u_ffn = resid_norm(h_mix, y, g_m, gn_f, sc_f, sh_f, BF16, f"norm_ffn_{i}")
        nxt = ffn_local(i + 1) if i + 1 < L else []
        gp, y_ffn, *spread = ffn_fwd(u_ffn, wgu8[i], wd8[i], f"ffn_fwd_{i}", spread=nxt)
        if nxt:
            wgu8[i + 1] = gather_pass(spread[0], f"ag_pass_ffn_gate_up_{i + 1}")
            wd8[i + 1] = gather_pass(spread[1], f"ag_pass_ffn_down_{i + 1}")
        saved.append((h_mix, mix, y, h_ffn, u_ffn, gp, y_ffn))
        h, y_prev, gate_prev = h_ffn, y_ffn, g_f

    dh, dfg, loss_part = final_loss(h, y_prev, gate_prev, final_g.reshape(1, D), loss_target[0], "final_loss")
    loss = lax.psum(loss_part[0, 0], ("x", "y", "c"))

    pending, ffn_recv = [], {}
    dpw, dps, dmod, dgn_mix, dgn_ffn = {}, {}, [None] * L, [None] * L, [None] * L
    for i in reversed(range(L)):
        sh_m, sc_m, g_m, sh_f, sc_f, g_f = (mod_vec(i, k) for k in range(N_MOD))
        kind, j = i % 3, i // 3
        gn_m, gn_f = norm_mix_g[i].reshape(1, D), norm_ffn_g[i].reshape(1, D)
        h_mix, mix, y, h_ffn, u_ffn, gp, y_ffn = saved[i]

        dy, dg_f = gate_bwd(dh, y_ffn, g_f, f"gate_bwd_ffn_{i}")
        a8, dgp = ffn_bwd_act(dy, gp, wd8[i], f"ffn_bwd_act_{i}")
        if pending:
            du, *ffn_recv[i + 1] = ffn_bwd_x(dgp, wgu8[i], f"ffn_bwd_x_{i}", carry=("chip", pending))
        else:
            du = ffn_bwd_x(dgp, wgu8[i], f"ffn_bwd_x_{i}")
        pending = []
        for tag, g8 in (("gate_up", ffn_bwd_wgu(u_ffn, dgp, f"ffn_bwd_wgu_{i}")), ("down", ffn_bwd_wd(a8, dy, f"ffn_bwd_wd_{i}"))):
            half = sibling_exchange(g8, f"d2d_ffn_{tag}_{i}")
            pending.append(pair_add(g8, half, f"pair_ffn_{tag}_{i}"))
        dh, dgn_ffn[i], dsc_f, dsh_f = norm_bwd(du, h_ffn, dh, gn_f, sc_f, f"norm_bwd_ffn_{i}")

        dy, dg_m = gate_bwd(dh, y, g_m, f"gate_bwd_mix_{i}")
        if kind == 0:
            (diff,) = mix
            dd, dpw[j], dps[j] = pool_bwd(dy, diff, pw[j], ps[j].reshape(1, D), f"pool_bwd_{i}")
            du = pool_window_bwd(dd, f"pool_window_bwd_{i}")
        elif kind == 1:
            u, U, q = mix
            dq = dense_bwd_x(dy, wcout, f"conv_out_bwd_x_{i}")
            dwcout = dense_bwd_w(q, dy, f"conv_out_bwd_w_{i}")
            dU, dcw = conv_bwd(dq, U, cw, f"conv_bwd_{i}")
            du = proj_bwd_x(dU, wcin8, f"conv_in_bwd_x_{i}")
            dwcin8 = proj_bwd_w(u, dU, f"conv_in_bwd_w_{i}")
        else:
            u, qkv, o, lt = mix
            do = dense_bwd_x(dy, wo, f"sb_out_bwd_x_{i}")
            dwo = dense_bwd_w(o, dy, f"sb_out_bwd_w_{i}")
            dq_, dk_, dv_ = attn_bwd(qkv, lt, do, f"attn_bwd_{i}")
            dqkv = jnp.concatenate([dq_, dk_, dv_], axis=1)
            du = proj_bwd_x(dqkv, wqkv8, f"sb_qkv_bwd_x_{i}")
            dwqkv8 = proj_bwd_w(u, dqkv, f"sb_qkv_bwd_w_{i}")
        dh, dgn_mix[i], dsc_m, dsh_m = norm_bwd(du, h_mix, dh, gn_m, sc_m, f"norm_bwd_mix_{i}")
        dmod[i] = jnp.concatenate([dsh_m, dsc_m, dg_m, dsh_f, dsc_f, dg_f], axis=1)
    grad_x = dh[None]

    def reduce_update(g8, w, m_, v_, name):
        recv = reduce_exchange(g8, name)
        w3 = w.reshape((-1,) + w.shape[-2:])
        recv = recv.reshape((N_CHIPS, 1) + w3.shape)
        outs = adam_update(recv, 0, w3, m_.reshape(w3.shape), v_.reshape(w3.shape), "adam_" + name)
        return [o_.reshape(w.shape) for o_ in outs]

    ffn_recv[0] = [chip_exchange(t, f"ici_ffn_{tag}_0") for tag, t in zip(("gate_up", "down"), pending)]
    up_gate = up_up = up_down = None
    for l in reversed(range(L)):
        r_gu, r_d = ffn_recv[l]
        r_gu = r_gu.reshape((N_CHIPS, 2, 1) + r_gu.shape[2:])
        r_d = r_d.reshape((N_CHIPS, 1, 1) + r_d.shape[1:])
        up_gate = adam_update(r_gu, 0, ffn_w_gate, m_ffn_w_gate, v_ffn_w_gate, f"adam_ffn_gate_{l}", layer=l, into=up_gate)
        up_up = adam_update(r_gu, 1, ffn_w_up, m_ffn_w_up, v_ffn_w_up, f"adam_ffn_up_{l}", layer=l, into=up_up)
        up_down = adam_update(r_d, 0, ffn_w_down, m_ffn_w_down, v_ffn_w_down, f"adam_ffn_down_{l}", layer=l, into=up_down)
    up_cin = reduce_update(dwcin8, conv_w_in, m_conv_w_in, v_conv_w_in, "conv_in")
    up_cout = reduce_update(dwcout.reshape(N_DEV, 1, dloc, D), conv_w_out, m_conv_w_out, v_conv_w_out, "conv_out")
    up_qkv = reduce_update(dwqkv8, sb_w_qkv, m_sb_w_qkv, v_sb_w_qkv, "sb_qkv")
    up_o = reduce_update(dwo.reshape(N_DEV, 1, dloc, D), sb_w_o, m_sb_w_o, v_sb_w_o, "sb_o")
    dpw_all = jnp.stack([dpw[j] for j in range(pool_w.shape[0])])
    dpw8 = dpw_all.reshape(-1, G, N_DEV, C // N_DEV, C).transpose(2, 0, 1, 3, 4).astype(BF16)
    up_pool = reduce_update(dpw8, pool_w, m_pool_w, v_pool_w, "pool_w")

    dps_all = jnp.concatenate([dps[j] for j in range(pool_scale.shape[0])], axis=0)
    dmod_loc = jnp.concatenate(dmod, axis=0)
    part_shapes = [(L, D), (L, D), (1, D), (L, N_MOD * D), dps_all.shape, (3, 1, D)]
    parts8 = all_gather(_pack([jnp.concatenate(dgn_mix, axis=0), jnp.concatenate(dgn_ffn, axis=0), dfg, dmod_loc,
                               dps_all, dcw]), "ag_small_out")
    g_mix, g_ffn, g_fin, g_bmod, g_ps, g_cw = _unpack(sum_devices(parts8, "sum_small"), part_shapes)
    g_ps = lax.dynamic_slice_in_dim(g_ps, me * dloc, dloc, axis=1)
    g_cw = lax.dynamic_slice_in_dim(g_cw.reshape(3, D), me * dloc, dloc, axis=1)
    dmod_all = _unpack(parts8, part_shapes, (N_DEV,))[3]
    dmod_cols = lax.dynamic_slice_in_dim(dmod_all, me * nmod, nmod, axis=2)
    dmod_pad = jnp.pad(dmod_cols.transpose(1, 0, 2), ((0, 0), (0, N_DEV), (0, 0)))
    up_wmod = mod_grad_adam(c_pad, dmod_pad, w_mod, m_w_mod, v_w_mod, "mod_grad_adam")

    small_w = [norm_mix_g, norm_ffn_g, final_g, b_mod, pool_scale, conv_w]
    small_m = [m_norm_mix_g, m_norm_ffn_g, m_final_g, m_b_mod, m_pool_scale, m_conv_w]
    small_v = [v_norm_mix_g, v_norm_ffn_g, v_final_g, v_b_mod, v_pool_scale, v_conv_w]
    small_g = [g_mix, g_ffn, g_fin.reshape(final_g.shape), g_bmod, g_ps, g_cw.reshape(conv_w.shape)]
    packed = [_pack(t)[None] for t in (small_w, small_m, small_v)]
    up_small = adam_update(_pack(small_g)[None, None, None], 0, *packed, "adam_small")
    sshapes = [t.shape for t in small_w]
    up_small = [_unpack(t[0], sshapes) for t in up_small]
    (s_mix, s_ffn, s_fin, s_bmod, s_ps, s_cw) = zip(*up_small)

    per_weight = [s_mix, s_ffn, up_wmod, s_bmod, up_pool, s_ps, up_cin, s_cw, up_cout, up_qkv, up_o,
                  up_gate, up_up, up_down, s_fin]
    outs = [loss, grad_x]
    for kind_idx in range(4):
        outs.extend(t[kind_idx] for t in per_weight)
    return tuple(outs)
```

```python
import functools

import jax
import jax.numpy as jnp
from jax import lax
from jax.experimental import pallas as pl
from jax.experimental.pallas import tpu as pltpu

F32, BF16 = jnp.float32, jnp.bfloat16
MESH = pl.DeviceIdType.MESH
N_DEV = 8
N_LAYERS = 4
N_MOD = 6
HEAD_DIM = 128
POOL_WINDOWS = (2, 4, 8, 16)
EPS = 1e-6
ADAM_LR, ADAM_B1, ADAM_B2, ADAM_EPS, ADAM_WD, ADAM_STEP = 0.001, 0.9, 0.999, 1e-08, 0.01, 10

LANES = 128
HALO = 16
ROW_TILE = 512
CONV_TILE = 256
MM_TILE = 1024
FFN_TILE = 512
ATT_K = 256
ATT_Q = 1024
ATT_Q_BWD = 512
ADAM_BLOCK = 256 * 1024
VMEM_LIMIT = 56 * 1024 * 1024

_NN = (((1,), (0,)), ((), ()))
_NT = (((1,), (1,)), ((), ()))
_TN = (((0,), (0,)), ((), ()))
_DIMS = {"nn": _NN, "nt": _NT, "tn": _TN}


def _cp(*sem):
    return pltpu.CompilerParams(dimension_semantics=sem if sem else None, vmem_limit_bytes=VMEM_LIMIT)


def _dot(a, b, dims=_NN):
    return lax.dot_general(a, b, dims, preferred_element_type=F32)


def _ld(ref, nlead):
    return ref[...] if nlead == 0 else ref[(0,) * nlead]


def _st(ref, nlead, val):
    if nlead == 0:
        ref[...] = val
    else:
        ref[(0,) * nlead] = val


def _row_tile(rows, cap, mult=8):
    if rows <= cap:
        return rows
    t = cap - cap % mult
    while rows % t:
        t -= mult
    return t


def _vec_spec(d, nidx):
    zero = (0, 0)
    return pl.BlockSpec((1, d), {1: lambda i: zero, 2: lambda i, j: zero}[nidx])


def _my_index():
    return 4 * lax.axis_index("x") + 2 * lax.axis_index("y") + lax.axis_index("c")


_HBM = pl.BlockSpec(memory_space=pl.ANY)
N_CHIPS = N_DEV // 2


def _remote(src, dst, send_sem, recv_sem, to):
    return pltpu.make_async_remote_copy(src_ref=src, dst_ref=dst, send_sem=send_sem, recv_sem=recv_sem,
                                        device_id=to, device_id_type=MESH)


def _dma_sems(n):
    return [pltpu.SemaphoreType.DMA((n,)), pltpu.SemaphoreType.DMA((n,)), pltpu.SemaphoreType.DMA((1,))]


def all_gather(x, name):
    def body(x_ref, o_ref, send_sems, recv_sems, local_sem):
        x, y, c = lax.axis_index("x"), lax.axis_index("y"), lax.axis_index("c")
        me, sibling = (x, y, c), (x, y, 1 - c)
        chips = [(1 - x, y), (x, 1 - y), (1 - x, 1 - y)]

        def slot(px, py, pc):
            return o_ref.at[4 * px + 2 * py + pc]

        def copy(k, block, to, src=None):
            return _remote(slot(*block) if src is None else src, slot(*block), send_sems.at[k], recv_sems.at[k], to)

        mine = pltpu.make_async_copy(x_ref, slot(*me), local_sem.at[0])
        mine.start()
        first = [copy(0, me, sibling, src=x_ref)]
        first += [copy(1 + j, me, (*chip, c), src=x_ref) for j, chip in enumerate(chips)]
        for cp in first:
            cp.start()
        passed = [copy(4 + j, (*chip, c), sibling) for j, chip in enumerate(chips)]
        for j, chip in enumerate(chips):
            copy(1 + j, (*chip, c), me).wait_recv()
            passed[j].start()
        copy(0, sibling, me).wait_recv()
        for j, chip in enumerate(chips):
            copy(4 + j, (*chip, 1 - c), me).wait_recv()
        for cp in first + passed:
            cp.wait_send()
        mine.wait()

    return pl.pallas_call(
        body, name=name, out_shape=jax.ShapeDtypeStruct((N_DEV,) + x.shape, x.dtype),
        in_specs=[_HBM], out_specs=_HBM, scratch_shapes=_dma_sems(N_DEV - 1),
    )(x)


def sibling_exchange(g8, name):
    def body(x_ref, o_ref, send_sems, recv_sems, _):
        x, y, c = lax.axis_index("x"), lax.axis_index("y"), lax.axis_index("c")
        copies = [_remote(x_ref.at[2 * k + 1 - c], o_ref.at[k], send_sems.at[k], recv_sems.at[k], (x, y, 1 - c))
                  for k in range(N_CHIPS)]
        for cp in copies:
            cp.start()
        for cp in copies:
            cp.wait_recv()
        for cp in copies:
            cp.wait_send()

    return pl.pallas_call(
        body, name=name, out_shape=jax.ShapeDtypeStruct((N_CHIPS,) + g8.shape[1:], g8.dtype),
        in_specs=[_HBM], out_specs=_HBM, scratch_shapes=_dma_sems(N_CHIPS),
    )(g8)


def pair_add(g8, half, name):
    lead = g8.shape[1:]
    C = lead[-1]
    M = 1
    for d in lead[:-1]:
        M *= d
    TR = _row_tile(M, max(16, ADAM_BLOCK // C), 16)

    def body(c_ref, g_ref, h_ref, o_ref):
        o_ref[...] = (g_ref[...].astype(F32) + h_ref[...].astype(F32)).astype(BF16)

    blk = (1, TR, C)
    grid_spec = pltpu.PrefetchScalarGridSpec(
        num_scalar_prefetch=1, grid=(N_CHIPS, M // TR),
        in_specs=[pl.BlockSpec(blk, lambda k, i, c_ref: (2 * k + c_ref[0], i, 0)), pl.BlockSpec(blk, lambda k, i, c_ref: (k, i, 0))],
        out_specs=pl.BlockSpec(blk, lambda k, i, c_ref: (k, i, 0)))
    out = pl.pallas_call(
        body, name=name, grid_spec=grid_spec, out_shape=jax.ShapeDtypeStruct((N_CHIPS, M, C), BF16),
        compiler_params=_cp("parallel", "parallel"),
    )(lax.axis_index("c").astype(jnp.int32).reshape(1), g8.reshape(N_DEV, M, C), half.reshape(N_CHIPS, M, C))
    return out.reshape((N_CHIPS,) + lead)


def chip_exchange(t, name):
    def body(x_ref, o_ref, send_sems, recv_sems, local_sem):
        x, y, c = lax.axis_index("x"), lax.axis_index("y"), lax.axis_index("c")
        mychip = 2 * x + y
        chips = [(1 - x, y), (x, 1 - y), (1 - x, 1 - y)]
        mine = pltpu.make_async_copy(x_ref.at[mychip], o_ref.at[mychip], local_sem.at[0])
        mine.start()
        sends = []
        for j, (px, py) in enumerate(chips):
            cp = _remote(x_ref.at[2 * px + py], o_ref.at[mychip], send_sems.at[j], recv_sems.at[j], (px, py, c))
            cp.start()
            sends.append(cp)
        for j, (px, py) in enumerate(chips):
            _remote(x_ref.at[2 * px + py], o_ref.at[2 * px + py], send_sems.at[j], recv_sems.at[j], (px, py, c)).wait_recv()
        for cp in sends:
            cp.wait_send()
        mine.wait()

    return pl.pallas_call(
        body, name=name, out_shape=jax.ShapeDtypeStruct(t.shape, t.dtype),
        in_specs=[_HBM], out_specs=_HBM, scratch_shapes=_dma_sems(N_CHIPS - 1),
    )(t)


def reduce_exchange(g8, name):
    half = sibling_exchange(g8, "d2d_" + name)
    return chip_exchange(pair_add(g8, half, "pair_" + name), "ici_" + name)


_CARRY_COPIES = {"spread": 4, "chip": 3}


def _carry_shape(kind, x):
    return jax.ShapeDtypeStruct(((N_DEV,) + x.shape) if kind == "spread" else x.shape, x.dtype)


def _carry_scratch(kind, n):
    return [pltpu.SemaphoreType.DMA((n * _CARRY_COPIES[kind],)), pltpu.SemaphoreType.DMA((n * _CARRY_COPIES[kind],)),
            pltpu.SemaphoreType.DMA((n,))]


def _carry_copies(kind, x_refs, o_refs, send_sems, recv_sems, local_sems):
    x, y, c = lax.axis_index("x"), lax.axis_index("y"), lax.axis_index("c")
    chips = [(1 - x, y), (x, 1 - y), (1 - x, 1 - y)]
    sends, recvs, local = [], [], []
    for a, (x_ref, o_ref) in enumerate(zip(x_refs, o_refs)):
        base = a * _CARRY_COPIES[kind]
        if kind == "spread":
            me = 4 * x + 2 * y + c
            peers = [(x, y, 1 - c)] + [(px, py, c) for px, py in chips]
            src = [x_ref] * 4
            mine = [me] * 4
            theirs = [4 * px + 2 * py + pc for px, py, pc in peers]
            local.append(pltpu.make_async_copy(x_ref, o_ref.at[me], local_sems.at[a]))
        else:
            mychip = 2 * x + y
            peers = [(px, py, c) for px, py in chips]
            theirs = [2 * px + py for px, py in chips]
            src = [x_ref.at[t] for t in theirs]
            mine = [mychip] * 3
            local.append(pltpu.make_async_copy(x_ref.at[mychip], o_ref.at[mychip], local_sems.at[a]))
        for k, peer in enumerate(peers):
            sends.append(_remote(src[k], o_ref.at[mine[k]], send_sems.at[base + k], recv_sems.at[base + k], peer))
            recvs.append(_remote(src[k], o_ref.at[theirs[k]], send_sems.at[base + k], recv_sems.at[base + k], peer))
    return sends, recvs, local


def _carry_run(kind, x_refs, o_refs, sems, first, last):
    sends, recvs, local = _carry_copies(kind, x_refs, o_refs, *sems)

    @pl.when(first)
    def _():
        for cp in local + sends:
            cp.start()

    @pl.when(last)
    def _():
        for cp in recvs:
            cp.wait_recv()
        for cp in sends:
            cp.wait_send()
        for cp in local:
            cp.wait()


def gather_pass(o8, name):
    def body(i_ref, o_ref, send_sems, recv_sems, _):
        del i_ref
        x, y, c = lax.axis_index("x"), lax.axis_index("y"), lax.axis_index("c")
        chips = [(1 - x, y), (x, 1 - y), (1 - x, 1 - y)]
        sends = [_remote(o_ref.at[4 * px + 2 * py + c], o_ref.at[4 * px + 2 * py + c], send_sems.at[j], recv_sems.at[j],
                         (x, y, 1 - c)) for j, (px, py) in enumerate(chips)]
        for cp in sends:
            cp.start()
        for j, (px, py) in enumerate(chips):
            slot = o_ref.at[4 * px + 2 * py + 1 - c]
            _remote(slot, slot, send_sems.at[j], recv_sems.at[j], (x, y, 1 - c)).wait_recv()
        for cp in sends:
            cp.wait_send()

    return pl.pallas_call(
        body, name=name, out_shape=jax.ShapeDtypeStruct(o8.shape, o8.dtype), in_specs=[_HBM], out_specs=_HBM,
        scratch_shapes=_dma_sems(N_CHIPS - 1), input_output_aliases={0: 0},
    )(o8)


def mm(a, b, *, mode, grid, a_blk, a_map, b_blk, b_map, o_shape, o_blk, o_map, o_dtype, name, carry=None):
    nk = grid[2]
    na, nb, no = len(a_blk) - 2, len(b_blk) - 2, len(o_blk) - 2
    dims = _DIMS[mode]
    kind, carried = carry if carry is not None else (None, [])
    nc = len(carried)

    def body(*refs):
        a_ref, b_ref = refs[:2]
        o_ref = refs[2 + nc]
        scratch = refs[3 + 2 * nc:]
        if nc:
            pos = [pl.program_id(d) for d in range(3)]
            first = (pos[0] == 0) & (pos[1] == 0) & (pos[2] == 0)
            last = (pos[0] == grid[0] - 1) & (pos[1] == grid[1] - 1) & (pos[2] == grid[2] - 1)
            _carry_run(kind, refs[2:2 + nc], refs[3 + nc:3 + 2 * nc], scratch[-3:], first, last)
        p = _dot(_ld(a_ref, na), _ld(b_ref, nb), dims)
        if nk == 1:
            _st(o_ref, no, p.astype(o_dtype))
        else:
            acc = scratch[0]
            k = pl.program_id(2)

            @pl.when(k == 0)
            def _():
                acc[...] = p

            @pl.when(k > 0)
            def _():
                acc[...] += p

            @pl.when(k == nk - 1)
            def _():
                _st(o_ref, no, acc[...].astype(o_dtype))

    scratch = [pltpu.VMEM(tuple(o_blk[-2:]), F32)] if nk > 1 else []
    out = pl.pallas_call(
        body, name=name, grid=grid,
        out_shape=[jax.ShapeDtypeStruct(o_shape, o_dtype)] + [_carry_shape(kind, x) for x in carried],
        in_specs=[pl.BlockSpec(a_blk, a_map), pl.BlockSpec(b_blk, b_map)] + [_HBM] * nc,
        out_specs=[pl.BlockSpec(o_blk, o_map)] + [_HBM] * nc,
        scratch_shapes=scratch + (_carry_scratch(kind, nc) if nc else []),
        compiler_params=_cp(*((("arbitrary",) * 3) if nc else ("parallel", "parallel", "arbitrary"))),
    )(a, b, *carried)
    return out if nc else out[0]


def proj_fwd(u, w8, o_dtype, name):
    S, D = u.shape
    nc = w8.shape[-1]
    T = min(MM_TILE, S)
    return mm(u, w8, mode="nn", grid=(S // T, N_DEV, 1), a_blk=(T, D), a_map=lambda i, j, k: (i, 0),
              b_blk=(1, 1, D, nc), b_map=lambda i, j, k: (j, 0, 0, 0), o_shape=(S, N_DEV * nc), o_blk=(T, nc),
              o_map=lambda i, j, k: (i, j), o_dtype=o_dtype, name=name)


def proj_bwd_x(dy, w8, name):
    S = dy.shape[0]
    D, nc = w8.shape[-2:]
    T = min(MM_TILE, S)
    return mm(dy, w8, mode="nt", grid=(S // T, 1, N_DEV), a_blk=(T, nc), a_map=lambda i, j, k: (i, k),
              b_blk=(1, 1, D, nc), b_map=lambda i, j, k: (k, 0, 0, 0), o_shape=(S, D), o_blk=(T, D),
              o_map=lambda i, j, k: (i, 0), o_dtype=F32, name=name)


def proj_bwd_w(u, dy, name):
    S, D = u.shape
    nc = dy.shape[1] // N_DEV
    T = min(MM_TILE, S)
    return mm(u, dy, mode="tn", grid=(N_DEV, 1, S // T), a_blk=(T, D), a_map=lambda j, _, k: (k, 0),
              b_blk=(T, nc), b_map=lambda j, _, k: (k, j), o_shape=(N_DEV, 1, D, nc), o_blk=(1, 1, D, nc),
              o_map=lambda j, _, k: (j, 0, 0, 0), o_dtype=BF16, name=name)


def dense_fwd(a, w, name):
    S, D = a.shape
    T = min(MM_TILE, S)
    return mm(a, w, mode="nn", grid=(S // T, 1, 1), a_blk=(T, D), a_map=lambda i, j, k: (i, 0), b_blk=(D, D),
              b_map=lambda i, j, k: (0, 0), o_shape=(S, D), o_blk=(T, D), o_map=lambda i, j, k: (i, 0),
              o_dtype=BF16, name=name)


def dense_bwd_x(dy, w, name):
    S, D = dy.shape
    T = min(MM_TILE, S)
    return mm(dy, w, mode="nt", grid=(S // T, 1, 1), a_blk=(T, D), a_map=lambda i, j, k: (i, 0), b_blk=(D, D),
              b_map=lambda i, j, k: (0, 0), o_shape=(S, D), o_blk=(T, D), o_map=lambda i, j, k: (i, 0),
              o_dtype=BF16, name=name)


def dense_bwd_w(a, dy, name):
    S, D = a.shape
    T = min(MM_TILE, S)
    half = D // 2
    return mm(a, dy, mode="tn", grid=(1, 2, S // T), a_blk=(T, D), a_map=lambda i, j, k: (k, 0), b_blk=(T, half),
              b_map=lambda i, j, k: (k, j), o_shape=(D, D), o_blk=(D, half), o_map=lambda i, j, k: (0, j),
              o_dtype=BF16, name=name)


def _mod_cols(nloc):
    return 256 if nloc % 256 == 0 else nloc


def mod_project(c_pad, w_mod, name):
    L, D, nloc = w_mod.shape
    R = c_pad.shape[0]
    tn = _mod_cols(nloc)

    def body(c_ref, w_ref, o_ref):
        c = c_ref[...]
        s = (c / (1.0 + jnp.exp(-c))).astype(BF16)
        o_ref[0] = _dot(s, w_ref[0].astype(BF16))

    return pl.pallas_call(
        body, name=name, grid=(L, nloc // tn), out_shape=jax.ShapeDtypeStruct((L, R, nloc), F32),
        in_specs=[pl.BlockSpec((R, D), lambda l, j: (0, 0)), pl.BlockSpec((1, D, tn), lambda l, j: (l, 0, j))],
        out_specs=pl.BlockSpec((1, R, tn), lambda l, j: (l, 0, j)), compiler_params=_cp("parallel", "parallel"),
    )(c_pad, w_mod)


def _adam(w, g, m, v):
    m = ADAM_B1 * m + (1.0 - ADAM_B1) * g
    v = ADAM_B2 * v + (1.0 - ADAM_B2) * (g * g)
    m_hat = m / (1.0 - ADAM_B1 ** ADAM_STEP)
    v_hat = v / (1.0 - ADAM_B2 ** ADAM_STEP)
    delta = -ADAM_LR * (m_hat / (jnp.sqrt(v_hat) + ADAM_EPS) + ADAM_WD * w)
    return delta, m, v


def mod_grad_adam(c_pad, dmod_pad, w, m, v, name):
    L, D, nloc = w.shape
    R = c_pad.shape[0]
    tn = _mod_cols(nloc)

    def body(c_ref, d_ref, w_ref, m_ref, v_ref, g_ref, dl_ref, nm_ref, nv_ref):
        c = c_ref[...]
        s = (c / (1.0 + jnp.exp(-c))).astype(BF16)
        g = _dot(s, d_ref[0].astype(BF16), _TN)
        delta, nm, nv = _adam(w_ref[0], g, m_ref[0], v_ref[0])
        g_ref[0], dl_ref[0], nm_ref[0], nv_ref[0] = g, delta, nm, nv

    wspec = pl.BlockSpec((1, D, tn), lambda l, j: (l, 0, j))
    out = jax.ShapeDtypeStruct(w.shape, F32)
    return pl.pallas_call(
        body, name=name, grid=(L, nloc // tn), out_shape=(out,) * 4,
        in_specs=[pl.BlockSpec((R, D), lambda l, j: (0, 0)), pl.BlockSpec((1, R, tn), lambda l, j: (l, 0, j)),
                  wspec, wspec, wspec],
        out_specs=(wspec,) * 4, compiler_params=_cp("parallel", "parallel"),
    )(c_pad, dmod_pad, w, m, v)


def resid_norm(h_prev, y_prev, gate_prev, gn, sc, sh, u_dtype, name):
    S, D = h_prev.shape
    T = min(ROW_TILE, S)
    has_res = y_prev is not None

    def body(*refs):
        if has_res:
            h_ref, y_ref, gate_ref, gn_ref, sc_ref, sh_ref, ho_ref, u_ref = refs
            h = h_ref[...] + gate_ref[...] * y_ref[...].astype(F32)
            ho_ref[...] = h
        else:
            h_ref, gn_ref, sc_ref, sh_ref, u_ref = refs
            h = h_ref[...]
        r = lax.rsqrt(jnp.mean(h * h, axis=-1, keepdims=True) + EPS)
        u = (h * r) * gn_ref[...] * (1.0 + sc_ref[...]) + sh_ref[...]
        u_ref[...] = u.astype(u_dtype)

    row = pl.BlockSpec((T, D), lambda i: (i, 0))
    vec = _vec_spec(D, 1)
    if has_res:
        h, u = pl.pallas_call(
            body, name=name, grid=(S // T,), out_shape=(jax.ShapeDtypeStruct((S, D), F32), jax.ShapeDtypeStruct((S, D), u_dtype)),
            in_specs=[row, row, vec, vec, vec, vec], out_specs=(row, row), compiler_params=_cp("parallel"),
        )(h_prev, y_prev, gate_prev, gn, sc, sh)
        return h, u
    u = pl.pallas_call(
        body, name=name, grid=(S // T,), out_shape=jax.ShapeDtypeStruct((S, D), u_dtype),
        in_specs=[row, vec, vec, vec], out_specs=row, compiler_params=_cp("parallel"),
    )(h_prev, gn, sc, sh)
    return h_prev, u


def gate_bwd(dh, y, gate, name):
    S, D = dh.shape
    T = min(ROW_TILE, S)

    def body(dh_ref, y_ref, gate_ref, dy_ref, dg_ref):
        i = pl.program_id(0)
        d = dh_ref[...]
        dy_ref[...] = (d * gate_ref[...]).astype(BF16)
        part = jnp.sum(d * y_ref[...].astype(F32), axis=0, keepdims=True)

        @pl.when(i == 0)
        def _():
            dg_ref[...] = part

        @pl.when(i > 0)
        def _():
            dg_ref[...] += part

    row = pl.BlockSpec((T, D), lambda i: (i, 0))
    vec = _vec_spec(D, 1)
    return pl.pallas_call(
        body, name=name, grid=(S // T,), out_shape=(jax.ShapeDtypeStruct((S, D), BF16), jax.ShapeDtypeStruct((1, D), F32)),
        in_specs=[row, row, vec], out_specs=(row, vec), compiler_params=_cp("arbitrary"),
    )(dh, y, gate)


def norm_bwd(du, h, dh_res, gn, sc, name):
    S, D = h.shape
    T = min(ROW_TILE, S)

    def body(du_ref, h_ref, dr_ref, gn_ref, sc_ref, dh_ref, dgn_ref, dsc_ref, dsh_ref):
        i = pl.program_id(0)
        d = du_ref[...].astype(F32)
        hh = h_ref[...]
        r = lax.rsqrt(jnp.mean(hh * hh, axis=-1, keepdims=True) + EPS)
        n = hh * r
        gn_v = gn_ref[...]
        dng = d * (1.0 + sc_ref[...])
        dn = dng * gn_v
        dh_ref[...] = dr_ref[...] + r * (dn - n * jnp.mean(dn * n, axis=-1, keepdims=True))
        parts = (jnp.sum(dng * n, axis=0, keepdims=True), jnp.sum(d * (n * gn_v), axis=0, keepdims=True),
                 jnp.sum(d, axis=0, keepdims=True))

        @pl.when(i == 0)
        def _():
            for ref, part in zip((dgn_ref, dsc_ref, dsh_ref), parts):
                ref[...] = part

        @pl.when(i > 0)
        def _():
            for ref, part in zip((dgn_ref, dsc_ref, dsh_ref), parts):
                ref[...] += part

    row = pl.BlockSpec((T, D), lambda i: (i, 0))
    vec = _vec_spec(D, 1)
    vshape = jax.ShapeDtypeStruct((1, D), F32)
    return pl.pallas_call(
        body, name=name, grid=(S // T,), out_shape=(jax.ShapeDtypeStruct((S, D), F32), vshape, vshape, vshape),
        in_specs=[row, row, row, vec, vec], out_specs=(row, vec, vec, vec), compiler_params=_cp("arbitrary"),
    )(du, h, dh_res, gn, sc)


def final_loss(h_prev, y_prev, gate_prev, fg, target, name):
    S, D = h_prev.shape
    T = min(ROW_TILE, S)

    def body(h_ref, y_ref, gate_ref, fg_ref, t_ref, dh_ref, dfg_ref, loss_ref):
        i = pl.program_id(0)
        h = h_ref[...] + gate_ref[...] * y_ref[...].astype(F32)
        r = lax.rsqrt(jnp.mean(h * h, axis=-1, keepdims=True) + EPS)
        n = h * r
        g = fg_ref[...]
        err = n * g - t_ref[...]
        dout = err * (1.0 / D)
        dn = dout * g
        dh_ref[...] = r * (dn - n * jnp.mean(dn * n, axis=-1, keepdims=True))
        dfg = jnp.sum(dout * n, axis=0, keepdims=True)
        part = 0.5 * jnp.sum(jnp.mean(err * err, axis=-1, keepdims=True), axis=0, keepdims=True)
        part = jnp.broadcast_to(part, (1, LANES))

        @pl.when(i == 0)
        def _():
            dfg_ref[...] = dfg
            loss_ref[...] = part

        @pl.when(i > 0)
        def _():
            dfg_ref[...] += dfg
            loss_ref[...] += part

    row = pl.BlockSpec((T, D), lambda i: (i, 0))
    vec = _vec_spec(D, 1)
    return pl.pallas_call(
        body, name=name, grid=(S // T,),
        out_shape=(jax.ShapeDtypeStruct((S, D), F32), jax.ShapeDtypeStruct((1, D), F32), jax.ShapeDtypeStruct((1, LANES), F32)),
        in_specs=[row, row, vec, vec, row], out_specs=(row, vec, pl.BlockSpec((1, LANES), lambda i: (0, 0))),
        compiler_params=_cp("arbitrary"),
    )(h_prev, y_prev, gate_prev, fg, target)


def _prev_halo(T):
    return lambda i: (jnp.maximum(i * (T // HALO) - 1, 0), 0)


def _next_halo(T, S):
    return lambda i: (jnp.minimum((i + 1) * (T // HALO), S // HALO - 1), 0)


def pool_fwd(u, w, ps, name):
    S, D = u.shape
    T = min(ROW_TILE, S)
    C = D // len(POOL_WINDOWS)

    def body(uc_ref, up_ref, w_ref, ps_ref, diff_ref, y_ref):
        i = pl.program_id(0)
        t = lax.broadcasted_iota(jnp.int32, (T, 1), 0) + i * T
        for g, win in enumerate(POOL_WINDOWS):
            cols = slice(g * C, (g + 1) * C)
            cur = uc_ref[:, cols]
            prev = jnp.where(i > 0, up_ref[:, cols], 0.0)
            s = jnp.concatenate([prev, cur], axis=0)
            k = 1
            while k < win:
                s = s + pltpu.roll(s, k, 0)
                k *= 2
            cnt = jnp.minimum(t + 1, win).astype(F32)
            diff = (s[HALO:, :] / cnt - cur).astype(BF16)
            diff_ref[:, cols] = diff
            y_ref[:, cols] = (_dot(diff, w_ref[g]) * ps_ref[:, cols]).astype(BF16)

    row = pl.BlockSpec((T, D), lambda i: (i, 0))
    out = jax.ShapeDtypeStruct((S, D), BF16)
    return pl.pallas_call(
        body, name=name, grid=(S // T,), out_shape=(out, out),
        in_specs=[row, pl.BlockSpec((HALO, D), _prev_halo(T)), pl.BlockSpec(w.shape, lambda i: (0, 0, 0)), _vec_spec(D, 1)],
        out_specs=(row, row), compiler_params=_cp("parallel"),
    )(u, u, w, ps)


def pool_bwd(dy, diff, w, ps, name):
    S, D = dy.shape
    T = min(ROW_TILE, S)
    G = len(POOL_WINDOWS)
    C = D // G

    def body(dy_ref, diff_ref, w_ref, ps_ref, dd_ref, dw_ref, dps_ref):
        i = pl.program_id(0)
        for g in range(G):
            cols = slice(g * C, (g + 1) * C)
            diff = diff_ref[:, cols]
            d = dy_ref[:, cols].astype(F32)
            ypre = _dot(diff, w_ref[g])
            dps = jnp.sum(d * ypre, axis=0, keepdims=True)
            dyp = (d * ps_ref[:, cols]).astype(BF16)
            dd_ref[:, cols] = _dot(dyp, w_ref[g], _NT)
            dw = _dot(diff, dyp, _TN)

            @pl.when(i == 0)
            def _():
                dw_ref[g] = dw
                dps_ref[:, cols] = dps

            @pl.when(i > 0)
            def _():
                dw_ref[g] += dw
                dps_ref[:, cols] += dps

    row = pl.BlockSpec((T, D), lambda i: (i, 0))
    wspec = pl.BlockSpec(w.shape, lambda i: (0, 0, 0))
    return pl.pallas_call(
        body, name=name, grid=(S // T,),
        out_shape=(jax.ShapeDtypeStruct((S, D), F32), jax.ShapeDtypeStruct(w.shape, F32), jax.ShapeDtypeStruct((1, D), F32)),
        in_specs=[row, row, wspec, _vec_spec(D, 1)], out_specs=(row, wspec, _vec_spec(D, 1)),
        compiler_params=_cp("arbitrary"),
    )(dy, diff, w, ps)


def pool_window_bwd(dd, name):
    S, D = dd.shape
    T = min(ROW_TILE, S)
    C = D // len(POOL_WINDOWS)
    n = T + HALO
    last = S // T - 1

    def body(dc_ref, dn_ref, du_ref):
        i = pl.program_id(0)
        t = lax.broadcasted_iota(jnp.int32, (n, 1), 0) + i * T
        for g, win in enumerate(POOL_WINDOWS):
            cols = slice(g * C, (g + 1) * C)
            cur = dc_ref[:, cols]
            nxt = jnp.where(i < last, dn_ref[:, cols], 0.0)
            cnt = jnp.minimum(t + 1, win).astype(F32)
            s = jnp.concatenate([cur, nxt], axis=0) / cnt
            k = 1
            while k < win:
                s = s + pltpu.roll(s, n - k, 0)
                k *= 2
            du_ref[:, cols] = s[:T, :] - cur

    row = pl.BlockSpec((T, D), lambda i: (i, 0))
    return pl.pallas_call(
        body, name=name, grid=(S // T,), out_shape=jax.ShapeDtypeStruct((S, D), F32),
        in_specs=[row, pl.BlockSpec((HALO, D), _next_halo(T, S))], out_specs=row, compiler_params=_cp("parallel"),
    )(dd, dd)


def _conv_chunk(D):
    return 512 if D % 512 == 0 else D


def conv_fwd(U, cw, name):
    S = U.shape[0]
    D = U.shape[1] // 3
    T = min(CONV_TILE, S)
    CH = _conv_chunk(D)

    def body(uc_ref, up_ref, cw_ref, q_ref):
        i = pl.program_id(0)
        for j in range(D // CH):
            cols = slice(j * CH, (j + 1) * CH)
            ccols = slice(D + j * CH, D + (j + 1) * CH)
            vcols = slice(2 * D + j * CH, 2 * D + (j + 1) * CH)
            zp = jnp.where(i > 0, up_ref[:, ccols] * up_ref[:, vcols], 0.0)
            z = jnp.concatenate([zp, uc_ref[:, ccols] * uc_ref[:, vcols]], axis=0)
            zc = cw_ref[2, :, cols] * z + cw_ref[1, :, cols] * pltpu.roll(z, 1, 0) + cw_ref[0, :, cols] * pltpu.roll(z, 2, 0)
            q_ref[:, cols] = (uc_ref[:, cols] * zc[HALO:, :]).astype(BF16)

    return pl.pallas_call(
        body, name=name, grid=(S // T,), out_shape=jax.ShapeDtypeStruct((S, D), BF16),
        in_specs=[pl.BlockSpec((T, 3 * D), lambda i: (i, 0)), pl.BlockSpec((HALO, 3 * D), _prev_halo(T)),
                  pl.BlockSpec((3, 1, D), lambda i: (0, 0, 0))],
        out_specs=pl.BlockSpec((T, D), lambda i: (i, 0)), compiler_params=_cp("parallel"),
    )(U, U, cw)


def conv_bwd(dq, U, cw, name):
    S = U.shape[0]
    D = U.shape[1] // 3
    T = min(CONV_TILE, S)
    CH = _conv_chunk(D)
    n = T + HALO
    last = S // T - 1

    def body(dq_ref, dqn_ref, uc_ref, up_ref, un_ref, cw_ref, du_ref, dcw_ref):
        i = pl.program_id(0)
        for j in range(D // CH):
            cols = slice(j * CH, (j + 1) * CH)
            ccols = slice(D + j * CH, D + (j + 1) * CH)
            vcols = slice(2 * D + j * CH, 2 * D + (j + 1) * CH)
            w0, w1, w2 = cw_ref[0, :, cols], cw_ref[1, :, cols], cw_ref[2, :, cols]
            b, c, v = uc_ref[:, cols], uc_ref[:, ccols], uc_ref[:, vcols]
            dqc = dq_ref[:, cols].astype(F32)
            zp = jnp.where(i > 0, up_ref[:, ccols] * up_ref[:, vcols], 0.0)
            z = jnp.concatenate([zp, c * v], axis=0)
            z1 = pltpu.roll(z, 1, 0)[HALO:, :]
            z2 = pltpu.roll(z, 2, 0)[HALO:, :]
            z0 = z[HALO:, :]
            zc = w2 * z0 + w1 * z1 + w0 * z2
            dzc = dqc * b
            dzn = jnp.where(i < last, dqn_ref[:, cols].astype(F32) * un_ref[:, cols], 0.0)
            e = jnp.concatenate([dzc, dzn], axis=0)
            dz = (w2 * e + w1 * pltpu.roll(e, n - 1, 0) + w0 * pltpu.roll(e, n - 2, 0))[:T, :]
            du_ref[:, cols] = (dqc * zc).astype(BF16)
            du_ref[:, ccols] = (dz * v).astype(BF16)
            du_ref[:, vcols] = (dz * c).astype(BF16)
            parts = [jnp.sum(dzc * zz, axis=0, keepdims=True) for zz in (z2, z1, z0)]

            @pl.when(i == 0)
            def _():
                for k in range(3):
                    dcw_ref[k, :, cols] = parts[k]

            @pl.when(i > 0)
            def _():
                for k in range(3):
                    dcw_ref[k, :, cols] += parts[k]

    return pl.pallas_call(
        body, name=name, grid=(S // T,),
        out_shape=(jax.ShapeDtypeStruct((S, 3 * D), BF16), jax.ShapeDtypeStruct((3, 1, D), F32)),
        in_specs=[pl.BlockSpec((T, D), lambda i: (i, 0)), pl.BlockSpec((HALO, D), _next_halo(T, S)),
                  pl.BlockSpec((T, 3 * D), lambda i: (i, 0)), pl.BlockSpec((HALO, 3 * D), _prev_halo(T)),
                  pl.BlockSpec((HALO, 3 * D), _next_halo(T, S)), pl.BlockSpec((3, 1, D), lambda i: (0, 0, 0))],
        out_specs=(pl.BlockSpec((T, 3 * D), lambda i: (i, 0)), pl.BlockSpec((3, 1, D), lambda i: (0, 0, 0))),
        compiler_params=_cp("arbitrary"),
    )(dq, dq, U, U, U, cw)


MASKED_LOG = -1e30
UNDERFLOW_LOG = -105.0


def _sb_logits(z, q0, k0, masked):
    z = z * (HEAD_DIM ** -0.5)
    lb = jnp.minimum(z, 0.0) - jnp.log(1.0 + jnp.exp(-jnp.abs(z)))
    l1 = lb - z
    if masked:
        causal = (lax.broadcasted_iota(jnp.int32, z.shape, 1) + k0) < (lax.broadcasted_iota(jnp.int32, z.shape, 0) + q0)
        lb = jnp.where(causal, lb, MASKED_LOG)
        l1 = jnp.where(causal, l1, 0.0)
    return lb, l1


def _key_sums(l1, tri):
    hi = l1.astype(BF16)
    lo = (l1 - hi.astype(F32)).astype(BF16)
    return _dot(hi, tri) + _dot(lo, tri)


def _tri(B, cmp):
    row = lax.broadcasted_iota(jnp.int32, (B, B), 0)
    col = lax.broadcasted_iota(jnp.int32, (B, B), 1)
    return cmp(row, col).astype(BF16)


def attn_fwd(qkv, name):
    S = qkv.shape[0]
    D = qkv.shape[1] // 3
    H = D // HEAD_DIM
    BK = min(ATT_K, S)
    BQ = min(ATT_Q, S)
    ND = BQ // BK
    R = BQ // min(ATT_Q_BWD, S)

    def body(q_ref, k_ref, v_ref, o_ref, lt_ref, first_ref):
        later = _tri(BK, lambda j, s: j > s)
        h = pl.program_id(0)

        def q_loop(qi, _):
            q0 = pl.multiple_of(qi * BQ, BQ)
            qb = q_ref[pl.ds(q0, BQ), :]
            n_off = qi * ND

            def scores(kb):
                k0 = pl.multiple_of(kb * BK, BK)
                return _dot(qb, k_ref[pl.ds(k0, BK), :], _NT)

            def accumulate(kb, z, masked, carry, oacc):
                k0 = pl.multiple_of(kb * BK, BK)
                lb, l1 = _sb_logits(z, q0, k0, masked)
                suffix = _key_sums(l1, later) + carry
                a = jnp.exp(lb + suffix)
                oacc = oacc + _dot(a.astype(BF16), v_ref[pl.ds(k0, BK), :])
                return carry + jnp.sum(l1, axis=1, keepdims=True), oacc

            carry, oacc = jnp.zeros((BQ, 1), F32), jnp.zeros((BQ, HEAD_DIM), F32)
            z = scores(n_off + ND - 1)
            for d in reversed(range(ND)):
                z_next = scores(jnp.maximum(n_off + d - 1, 0))
                carry, oacc = accumulate(n_off + d, z, True, carry, oacc)
                z = z_next

            def live(carry):
                return (jnp.max(carry) >= UNDERFLOW_LOG).astype(jnp.int32)

            def k_cond(state):
                t, go = state[0], state[1]
                return (t < n_off) & (go > 0)

            def k_body(state):
                t, _, carry, oacc, z = state
                kb = n_off - 1 - t
                z_next = scores(jnp.maximum(kb - 1, 0))
                carry, oacc = accumulate(kb, z, False, carry, oacc)
                return t + 1, live(carry), carry, oacc, z_next

            visited, _, total, oacc, _ = lax.while_loop(k_cond, k_body, (jnp.int32(0), live(carry), carry, oacc, z))
            o_ref[pl.ds(q0, BQ), :] = oacc.astype(BF16)
            lt_ref[pl.ds(q0, BQ), :] = jnp.broadcast_to(total, (BQ, HEAD_DIM))
            for r in range(R):
                first_ref[h, qi * R + r] = n_off - visited
            return 0

        lax.fori_loop(0, S // BQ, q_loop, 0)

    def head(off):
        return pl.BlockSpec((S, HEAD_DIM), lambda h: (0, off + h))

    return pl.pallas_call(
        body, name=name, grid=(H,),
        out_shape=(jax.ShapeDtypeStruct((S, D), BF16), jax.ShapeDtypeStruct((S, D), F32),
                   jax.ShapeDtypeStruct((H, S // min(ATT_Q_BWD, S)), jnp.int32)),
        in_specs=[head(0), head(H), head(2 * H)],
        out_specs=(head(0), head(0), pl.BlockSpec(memory_space=pltpu.SMEM)), compiler_params=_cp("arbitrary"),
    )(qkv, qkv, qkv)


def attn_bwd(qkv, lt, first, do, name):
    S = qkv.shape[0]
    D = qkv.shape[1] // 3
    H = D // HEAD_DIM
    BK = min(ATT_K, S)
    BQ = min(ATT_Q_BWD, S)
    ND = BQ // BK
    scale = HEAD_DIM ** -0.5

    def body(q_ref, k_ref, v_ref, lt_ref, first_ref, do_ref, dq_ref, dk_ref, dv_ref, dk_acc, dv_acc):
        upto = _tri(BK, lambda j, s: j <= s)
        before = _tri(BK, lambda j, s: j < s)
        dk_acc[...] = jnp.zeros_like(dk_acc)
        dv_acc[...] = jnp.zeros_like(dv_acc)
        h = pl.program_id(0)

        def q_loop(qi, _):
            q0 = pl.multiple_of(qi * BQ, BQ)
            qb = q_ref[pl.ds(q0, BQ), :]
            dob = do_ref[pl.ds(q0, BQ), :]
            total = lt_ref[pl.ds(q0, BQ), :][:, :1]
            n_off = qi * ND
            first = first_ref[h, qi]

            def scores(kb):
                k0 = pl.multiple_of(kb * BK, BK)
                return _dot(qb, k_ref[pl.ds(k0, BK), :], _NT), _dot(dob, v_ref[pl.ds(k0, BK), :], _NT)

            def accumulate(kb, z, da, masked, lsum, gsum, dqacc):
                k0 = pl.multiple_of(kb * BK, BK)
                lb, l1 = _sb_logits(z, q0, k0, masked)
                suffix = total - lsum - _key_sums(l1, upto)
                a = jnp.exp(lb + suffix)
                g = a * da
                gpre = gsum + _dot(g.astype(BF16), before)
                beta = jnp.exp(lb)
                dz = ((g * (1.0 - beta) - gpre * beta) * scale).astype(BF16)
                dqacc = dqacc + _dot(dz, k_ref[pl.ds(k0, BK), :])
                dk_acc[pl.ds(k0, BK), :] += _dot(dz, qb, _TN)
                dv_acc[pl.ds(k0, BK), :] += _dot(a.astype(BF16), dob, _TN)
                return (lsum + jnp.sum(l1, axis=1, keepdims=True), gsum + jnp.sum(g, axis=1, keepdims=True), dqacc)

            def k_loop(t, state):
                lsum, gsum, dqacc, z, da = state
                z_next, da_next = scores(t + 1)
                lsum, gsum, dqacc = accumulate(t, z, da, False, lsum, gsum, dqacc)
                return lsum, gsum, dqacc, z_next, da_next

            zero = jnp.zeros((BQ, 1), F32)
            lsum, gsum, dqacc, z, da = lax.fori_loop(
                first, n_off, k_loop, (zero, zero, jnp.zeros((BQ, HEAD_DIM), F32), *scores(first)))
            for d in range(ND):
                nxt = scores(n_off + d + 1) if d + 1 < ND else None
                lsum, gsum, dqacc = accumulate(n_off + d, z, da, True, lsum, gsum, dqacc)
                if nxt is not None:
                    z, da = nxt
            dq_ref[pl.ds(q0, BQ), :] = dqacc.astype(BF16)
            return 0

        lax.fori_loop(0, S // BQ, q_loop, 0)
        dk_ref[...] = dk_acc[...].astype(BF16)
        dv_ref[...] = dv_acc[...].astype(BF16)

    def head(off):
        return pl.BlockSpec((S, HEAD_DIM), lambda h: (0, off + h))

    out = jax.ShapeDtypeStruct((S, D), BF16)
    return pl.pallas_call(
        body, name=name, grid=(H,), out_shape=(out, out, out),
        in_specs=[head(0), head(H), head(2 * H), head(0), pl.BlockSpec(memory_space=pltpu.SMEM), head(0)],
        out_specs=(head(0), head(0), head(0)),
        scratch_shapes=[pltpu.VMEM((S, HEAD_DIM), F32), pltpu.VMEM((S, HEAD_DIM), F32)], compiler_params=_cp("parallel"),
    )(qkv, qkv, qkv, lt, first, do)


def ffn_fwd(u, wgu8, wd8, name, spread=()):
    S, D = u.shape
    n = wgu8.shape[-1]
    T = min(FFN_TILE, S)
    nc = len(spread)
    nsteps = S // T

    def body(*refs):
        u_ref, wgu_ref, wd_ref = refs[:3]
        gp_ref, y_ref = refs[3 + nc:5 + nc]
        acc = refs[5 + 2 * nc]
        i, j = pl.program_id(0), pl.program_id(1)
        if nc:
            _carry_run("spread", refs[3:3 + nc], refs[5 + nc:5 + 2 * nc], refs[6 + 2 * nc:],
                       (i == 0) & (j == 0), (i == nsteps - 1) & (j == N_DEV - 1))
        uu = u_ref[...]
        g = _dot(uu, wgu_ref[0, 0])
        p = _dot(uu, wgu_ref[0, 1])
        gp_ref[0, 0] = g.astype(BF16)
        gp_ref[1, 0] = p.astype(BF16)
        a = (g / (1.0 + jnp.exp(-g)) * p).astype(BF16)
        part = _dot(a, wd_ref[0])

        @pl.when(j == 0)
        def _():
            acc[...] = part

        @pl.when(j > 0)
        def _():
            acc[...] += part

        @pl.when(j == N_DEV - 1)
        def _():
            y_ref[...] = acc[...].astype(BF16)

    return pl.pallas_call(
        body, name=name, grid=(nsteps, N_DEV),
        out_shape=[jax.ShapeDtypeStruct((2, N_DEV, S, n), BF16), jax.ShapeDtypeStruct((S, D), BF16)]
        + [_carry_shape("spread", x) for x in spread],
        in_specs=[pl.BlockSpec((T, D), lambda i, j: (i, 0)), pl.BlockSpec((1, 2, D, n), lambda i, j: (j, 0, 0, 0)),
                  pl.BlockSpec((1, n, D), lambda i, j: (j, 0, 0))] + [_HBM] * nc,
        out_specs=[pl.BlockSpec((2, 1, T, n), lambda i, j: (0, j, i, 0)), pl.BlockSpec((T, D), lambda i, j: (i, 0))]
        + [_HBM] * nc,
        scratch_shapes=[pltpu.VMEM((T, D), F32)] + (_carry_scratch("spread", nc) if nc else []),
        compiler_params=_cp("arbitrary" if nc else "parallel", "arbitrary"),
    )(u, wgu8, wd8, *spread)


def ffn_bwd_act(dy, gp, wd8, name):
    S, D = dy.shape
    n = gp.shape[-1]
    T = min(FFN_TILE, S)

    def body(dy_ref, gp_ref, wd_ref, a_ref, dgp_ref):
        da = _dot(dy_ref[...], wd_ref[0], _NT)
        g = gp_ref[0, 0].astype(F32)
        p = gp_ref[1, 0].astype(F32)
        s = 1.0 / (1.0 + jnp.exp(-g))
        sl = g * s
        a_ref[0] = (sl * p).astype(BF16)
        dgp_ref[0, 0] = (da * p * (s * (1.0 + g * (1.0 - s)))).astype(BF16)
        dgp_ref[1, 0] = (da * sl).astype(BF16)

    gp_spec = pl.BlockSpec((2, 1, T, n), lambda i, j: (0, j, i, 0))
    return pl.pallas_call(
        body, name=name, grid=(S // T, N_DEV),
        out_shape=(jax.ShapeDtypeStruct((N_DEV, S, n), BF16), jax.ShapeDtypeStruct((2, N_DEV, S, n), BF16)),
        in_specs=[pl.BlockSpec((T, D), lambda i, j: (i, 0)), gp_spec, pl.BlockSpec((1, n, D), lambda i, j: (j, 0, 0))],
        out_specs=(pl.BlockSpec((1, T, n), lambda i, j: (j, i, 0)), gp_spec), compiler_params=_cp("parallel", "parallel"),
    )(dy, gp, wd8)


def ffn_bwd_x(dgp, wgu8, name, carry=None):
    S, n = dgp.shape[-2:]
    D = wgu8.shape[-2]
    T = min(MM_TILE, S)
    return mm(dgp, wgu8, mode="nt", grid=(S // T, 1, 2 * N_DEV), a_blk=(1, 1, T, n),
              a_map=lambda i, j, k: (k // N_DEV, k % N_DEV, i, 0), b_blk=(1, 1, D, n),
              b_map=lambda i, j, k: (k % N_DEV, k // N_DEV, 0, 0), o_shape=(S, D), o_blk=(T, D),
              o_map=lambda i, j, k: (i, 0), o_dtype=F32, name=name, carry=carry)


def ffn_bwd_wgu(u, dgp, name):
    S, D = u.shape
    n = dgp.shape[-1]
    T = min(MM_TILE, S)
    return mm(u, dgp, mode="tn", grid=(N_DEV, 2, S // T), a_blk=(T, D), a_map=lambda j, t, k: (k, 0),
              b_blk=(1, 1, T, n), b_map=lambda j, t, k: (t, j, k, 0), o_shape=(N_DEV, 2, D, n), o_blk=(1, 1, D, n),
              o_map=lambda j, t, k: (j, t, 0, 0), o_dtype=BF16, name=name)


def ffn_bwd_wd(a8, dy, name):
    n = a8.shape[-1]
    S, D = dy.shape
    T = min(MM_TILE, S)
    return mm(a8, dy, mode="tn", grid=(N_DEV, 1, S // T), a_blk=(1, T, n), a_map=lambda j, _, k: (j, k, 0),
              b_blk=(T, D), b_map=lambda j, _, k: (k, 0), o_shape=(N_DEV, n, D), o_blk=(1, n, D),
              o_map=lambda j, _, k: (j, 0, 0), o_dtype=BF16, name=name)


def adam_update(recv, a_idx, w, m, v, name, layer=None, into=None):
    L, R, C = w.shape
    TR = _row_tile(R, max(8, ADAM_BLOCK // C))
    nsrc = recv.shape[0]

    def body(r_ref, w_ref, m_ref, v_ref, *rest):
        g_ref, d_ref, nm_ref, nv_ref = rest[-4:]
        g = r_ref[0, 0, 0].astype(F32)
        for k in range(1, nsrc):
            g = g + r_ref[k, 0, 0].astype(F32)
        delta, nm, nv = _adam(w_ref[0], g, m_ref[0], v_ref[0])
        g_ref[0], d_ref[0], nm_ref[0], nv_ref[0] = g, delta, nm, nv

    if layer is None:
        grid, wmap, rmap = (L, R // TR), (lambda l, i: (l, i, 0)), (lambda l, i: (0, a_idx, l, i, 0))
    else:
        grid, wmap, rmap = (1, R // TR), (lambda l, i: (layer, i, 0)), (lambda l, i: (0, a_idx, 0, i, 0))
    wspec = pl.BlockSpec((1, TR, C), wmap)
    out = jax.ShapeDtypeStruct(w.shape, F32)
    prior = list(into) if into is not None else []
    return pl.pallas_call(
        body, name=name, grid=grid, out_shape=(out,) * 4,
        in_specs=[pl.BlockSpec((nsrc, 1, 1, TR, C), rmap), wspec, wspec, wspec] + [_HBM] * len(prior),
        out_specs=(wspec,) * 4, input_output_aliases={4 + k: k for k in range(len(prior))},
        compiler_params=_cp("parallel", "parallel"),
    )(recv, w, m, v, *prior)


def sum_devices(x, name):
    _, R, C = x.shape

    def body(x_ref, o_ref):
        s = x_ref[0]
        for k in range(1, N_DEV):
            s = s + x_ref[k]
        o_ref[...] = s

    return pl.pallas_call(
        body, name=name, out_shape=jax.ShapeDtypeStruct((R, C), F32),
        in_specs=[pl.BlockSpec(memory_space=pltpu.VMEM)], out_specs=pl.BlockSpec(memory_space=pltpu.VMEM),
    )(x)


def _pack(parts):
    flat = jnp.concatenate([p.reshape(-1) for p in parts])
    pad = (-flat.shape[0]) % (8 * LANES)
    return jnp.pad(flat, (0, pad)).reshape(-1, LANES)


def _unpack(packed, shapes, lead=()):
    flat = packed.reshape(lead + (-1,))
    out, off = [], 0
    for s in shapes:
        size = 1
        for d in s:
            size *= d
        out.append(flat[..., off:off + size].reshape(lead + tuple(s)))
        off += size
    return out


def kernel(x, c, norm_mix_g, norm_ffn_g, w_mod, b_mod, pool_w, pool_scale, conv_w_in, conv_w, conv_w_out, sb_w_qkv, sb_w_o, ffn_w_gate, ffn_w_up, ffn_w_down, final_g, loss_target, m_norm_mix_g, m_norm_ffn_g, m_w_mod, m_b_mod, m_pool_w, m_pool_scale, m_conv_w_in, m_conv_w, m_conv_w_out, m_sb_w_qkv, m_sb_w_o, m_ffn_w_gate, m_ffn_w_up, m_ffn_w_down, m_final_g, v_norm_mix_g, v_norm_ffn_g, v_w_mod, v_b_mod, v_pool_w, v_pool_scale, v_conv_w_in, v_conv_w, v_conv_w_out, v_sb_w_qkv, v_sb_w_o, v_ffn_w_gate, v_ffn_w_up, v_ffn_w_down, v_final_g):
    S, D = x.shape[1:]
    L = N_LAYERS
    G = len(POOL_WINDOWS)
    C = D // G
    dloc = D // N_DEV
    nmod = w_mod.shape[-1]
    me = _my_index()

    def ffn_local(l):
        return [jnp.stack([ffn_w_gate[l], ffn_w_up[l]]).astype(BF16), ffn_w_down[l].astype(BF16)]

    wgu8 = {0: all_gather(ffn_local(0)[0], "ag_ffn_gate_up_0")}
    wd8 = {0: all_gather(ffn_local(0)[1], "ag_ffn_down_0")}
    wcin8 = all_gather(conv_w_in.astype(BF16), "ag_conv_in")
    wcout = all_gather(conv_w_out.astype(BF16), "ag_conv_out").reshape(D, D)
    wqkv8 = all_gather(sb_w_qkv.astype(BF16), "ag_sb_qkv")
    wo = all_gather(sb_w_o.astype(BF16), "ag_sb_o").reshape(D, D)
    pw = all_gather(pool_w.astype(BF16), "ag_pool_w")
    pw = pw.transpose(1, 2, 0, 3, 4).reshape(pool_w.shape[0], G, C, C)
    small_shapes = [(1, D), (3, dloc), (pool_scale.shape[0], dloc)]
    small = all_gather(_pack([c, conv_w, pool_scale]), "ag_small_in")
    c_all, cw_all, ps_all = _unpack(small, small_shapes, (N_DEV,))
    c_pad = jnp.pad(c_all.reshape(N_DEV, D), ((0, N_DEV), (0, 0)))
    cw = cw_all.transpose(1, 0, 2).reshape(3, 1, D)
    ps = ps_all.transpose(1, 0, 2).reshape(-1, D)

    mod_part = mod_project(c_pad, w_mod, "mod_project")
    mod_all = all_gather(mod_part, "ag_mod")
    mod = lax.dynamic_index_in_dim(mod_all, me, axis=2, keepdims=False)
    mod = mod.transpose(1, 0, 2).reshape(L, N_MOD * D) + b_mod

    def mod_vec(i, k):
        return mod[i, k * D:(k + 1) * D].reshape(1, D)

    h = x[0]
    y_prev = gate_prev = None
    saved = []
    for i in range(L):
        sh_m, sc_m, g_m, sh_f, sc_f, g_f = (mod_vec(i, k) for k in range(N_MOD))
        kind, j = i % 3, i // 3
        gn_m, gn_f = norm_mix_g[i].reshape(1, D), norm_ffn_g[i].reshape(1, D)
        h_mix, u = resid_norm(h, y_prev, gate_prev, gn_m, sc_m, sh_m, F32 if kind == 0 else BF16, f"norm_mix_{i}")
        if kind == 0:
            diff, y = pool_fwd(u, pw[j], ps[j].reshape(1, D), f"pool_fwd_{i}")
            mix = (diff,)
        elif kind == 1:
            U = proj_fwd(u, wcin8, F32, f"conv_in_{i}")
            q = conv_fwd(U, cw, f"conv_fwd_{i}")
            y = dense_fwd(q, wcout, f"conv_out_{i}")
            mix = (u, U, q)
        else:
            qkv = proj_fwd(u, wqkv8, BF16, f"sb_qkv_{i}")
            o, lt, first = attn_fwd(qkv, f"attn_fwd_{i}")
            y = dense_fwd(o, wo, f"sb_out_{i}")
            mix = (u, qkv, o, lt, first)
        h_ffn, u_ffn = resid_norm(h_mix, y, g_m, gn_f, sc_f, sh_f, BF16, f"norm_ffn_{i}")
        nxt = ffn_local(i + 1) if i + 1 < L else []
        gp, y_ffn, *spread = ffn_fwd(u_ffn, wgu8[i], wd8[i], f"ffn_fwd_{i}", spread=nxt)
        if nxt:
            wgu8[i + 1] = gather_pass(spread[0], f"ag_pass_ffn_gate_up_{i + 1}")
            wd8[i + 1] = gather_pass(spread[1], f"ag_pass_ffn_down_{i + 1}")
        saved.append((h_mix, mix, y, h_ffn, u_ffn, gp, y_ffn))
        h, y_prev, gate_prev = h_ffn, y_ffn, g_f

    dh, dfg, loss_part = final_loss(h, y_prev, gate_prev, final_g.reshape(1, D), loss_target[0], "final_loss")
    loss = lax.psum(loss_part[0, 0], ("x", "y", "c"))

    pending, ffn_recv = [], {}
    dpw, dps, dmod, dgn_mix, dgn_ffn = {}, {}, [None] * L, [None] * L, [None] * L
    for i in reversed(range(L)):
        sh_m, sc_m, g_m, sh_f, sc_f, g_f = (mod_vec(i, k) for k in range(N_MOD))
        kind, j = i % 3, i // 3
        gn_m, gn_f = norm_mix_g[i].reshape(1, D), norm_ffn_g[i].reshape(1, D)
        h_mix, mix, y, h_ffn, u_ffn, gp, y_ffn = saved[i]

        dy, dg_f = gate_bwd(dh, y_ffn, g_f, f"gate_bwd_ffn_{i}")
        a8, dgp = ffn_bwd_act(dy, gp, wd8[i], f"ffn_bwd_act_{i}")
        if pending:
            du, *ffn_recv[i + 1] = ffn_bwd_x(dgp, wgu8[i], f"ffn_bwd_x_{i}", carry=("chip", pending))
        else:
            du = ffn_bwd_x(dgp, wgu8[i], f"ffn_bwd_x_{i}")
        pending = []
        for tag, g8 in (("gate_up", ffn_bwd_wgu(u_ffn, dgp, f"ffn_bwd_wgu_{i}")), ("down", ffn_bwd_wd(a8, dy, f"ffn_bwd_wd_{i}"))):
            half = sibling_exchange(g8, f"d2d_ffn_{tag}_{i}")
            pending.append(pair_add(g8, half, f"pair_ffn_{tag}_{i}"))
        dh, dgn_ffn[i], dsc_f, dsh_f = norm_bwd(du, h_ffn, dh, gn_f, sc_f, f"norm_bwd_ffn_{i}")

        dy, dg_m = gate_bwd(dh, y, g_m, f"gate_bwd_mix_{i}")
        if kind == 0:
            (diff,) = mix
            dd, dpw[j], dps[j] = pool_bwd(dy, diff, pw[j], ps[j].reshape(1, D), f"pool_bwd_{i}")
            du = pool_window_bwd(dd, f"pool_window_bwd_{i}")
        elif kind == 1:
            u, U, q = mix
            dq = dense_bwd_x(dy, wcout, f"conv_out_bwd_x_{i}")
            dwcout = dense_bwd_w(q, dy, f"conv_out_bwd_w_{i}")
            dU, dcw = conv_bwd(dq, U, cw, f"conv_bwd_{i}")
            du = proj_bwd_x(dU, wcin8, f"conv_in_bwd_x_{i}")
            dwcin8 = proj_bwd_w(u, dU, f"conv_in_bwd_w_{i}")
        else:
            u, qkv, o, lt, first = mix
            do = dense_bwd_x(dy, wo, f"sb_out_bwd_x_{i}")
            dwo = dense_bwd_w(o, dy, f"sb_out_bwd_w_{i}")
            dq_, dk_, dv_ = attn_bwd(qkv, lt, first, do, f"attn_bwd_{i}")
            dqkv = jnp.concatenate([dq_, dk_, dv_], axis=1)
            du = proj_bwd_x(dqkv, wqkv8, f"sb_qkv_bwd_x_{i}")
            dwqkv8 = proj_bwd_w(u, dqkv, f"sb_qkv_bwd_w_{i}")
        dh, dgn_mix[i], dsc_m, dsh_m = norm_bwd(du, h_mix, dh, gn_m, sc_m, f"norm_bwd_mix_{i}")
        dmod[i] = jnp.concatenate([dsh_m, dsc_m, dg_m, dsh_f, dsc_f, dg_f], axis=1)
    grad_x = dh[None]

    def reduce_update(g8, w, m_, v_, name):
        recv = reduce_exchange(g8, name)
        w3 = w.reshape((-1,) + w.shape[-2:])
        recv = recv.reshape((N_CHIPS, 1) + w3.shape)
        outs = adam_update(recv, 0, w3, m_.reshape(w3.shape), v_.reshape(w3.shape), "adam_" + name)
        return [o_.reshape(w.shape) for o_ in outs]

    ffn_recv[0] = [chip_exchange(t, f"ici_ffn_{tag}_0") for tag, t in zip(("gate_up", "down"), pending)]
    up_gate = up_up = up_down = None
    for l in reversed(range(L)):
        r_gu, r_d = ffn_recv[l]
        r_gu = r_gu.reshape((N_CHIPS, 2, 1) + r_gu.shape[2:])
        r_d = r_d.reshape((N_CHIPS, 1, 1) + r_d.shape[1:])
        up_gate = adam_update(r_gu, 0, ffn_w_gate, m_ffn_w_gate, v_ffn_w_gate, f"adam_ffn_gate_{l}", layer=l, into=up_gate)
        up_up = adam_update(r_gu, 1, ffn_w_up, m_ffn_w_up, v_ffn_w_up, f"adam_ffn_up_{l}", layer=l, into=up_up)
        up_down = adam_update(r_d, 0, ffn_w_down, m_ffn_w_down, v_ffn_w_down, f"adam_ffn_down_{l}", layer=l, into=up_down)
    up_cin = reduce_update(dwcin8, conv_w_in, m_conv_w_in, v_conv_w_in, "conv_in")
    up_cout = reduce_update(dwcout.reshape(N_DEV, 1, dloc, D), conv_w_out, m_conv_w_out, v_conv_w_out, "conv_out")
    up_qkv = reduce_update(dwqkv8, sb_w_qkv, m_sb_w_qkv, v_sb_w_qkv, "sb_qkv")
    up_o = reduce_update(dwo.reshape(N_DEV, 1, dloc, D), sb_w_o, m_sb_w_o, v_sb_w_o, "sb_o")
    dpw_all = jnp.stack([dpw[j] for j in range(pool_w.shape[0])])
    dpw8 = dpw_all.reshape(-1, G, N_DEV, C // N_DEV, C).transpose(2, 0, 1, 3, 4).astype(BF16)
    up_pool = reduce_update(dpw8, pool_w, m_pool_w, v_pool_w, "pool_w")

    dps_all = jnp.concatenate([dps[j] for j in range(pool_scale.shape[0])], axis=0)
    dmod_loc = jnp.concatenate(dmod, axis=0)
    part_shapes = [(L, D), (L, D), (1, D), (L, N_MOD * D), dps_all.shape, (3, 1, D)]
    parts8 = all_gather(_pack([jnp.concatenate(dgn_mix, axis=0), jnp.concatenate(dgn_ffn, axis=0), dfg, dmod_loc,
                               dps_all, dcw]), "ag_small_out")
    g_mix, g_ffn, g_fin, g_bmod, g_ps, g_cw = _unpack(sum_devices(parts8, "sum_small"), part_shapes)
    g_ps = lax.dynamic_slice_in_dim(g_ps, me * dloc, dloc, axis=1)
    g_cw = lax.dynamic_slice_in_dim(g_cw.reshape(3, D), me * dloc, dloc, axis=1)
    dmod_all = _unpack(parts8, part_shapes, (N_DEV,))[3]
    dmod_cols = lax.dynamic_slice_in_dim(dmod_all, me * nmod, nmod, axis=2)
    dmod_pad = jnp.pad(dmod_cols.transpose(1, 0, 2), ((0, 0), (0, N_DEV), (0, 0)))
    up_wmod = mod_grad_adam(c_pad, dmod_pad, w_mod, m_w_mod, v_w_mod, "mod_grad_adam")

    small_w = [norm_mix_g, norm_ffn_g, final_g, b_mod, pool_scale, conv_w]
    small_m = [m_norm_mix_g, m_norm_ffn_g, m_final_g, m_b_mod, m_pool_scale, m_conv_w]
    small_v = [v_norm_mix_g, v_norm_ffn_g, v_final_g, v_b_mod, v_pool_scale, v_conv_w]
    small_g = [g_mix, g_ffn, g_fin.reshape(final_g.shape), g_bmod, g_ps, g_cw.reshape(conv_w.shape)]
    packed = [_pack(t)[None] for t in (small_w, small_m, small_v)]
    up_small = adam_update(_pack(small_g)[None, None, None], 0, *packed, "adam_small")
    sshapes = [t.shape for t in small_w]
    up_small = [_unpack(t[0], sshapes) for t in up_small]
    (s_mix, s_ffn, s_fin, s_bmod, s_ps, s_cw) = zip(*up_small)

    per_weight = [s_mix, s_ffn, up_wmod, s_bmod, up_pool, s_ps, up_cin, s_cw, up_cout, up_qkv, up_o,
                  up_gate, up_up, up_down, s_fin]
    outs = [loss, grad_x]
    for kind_idx in range(4):
        outs.extend(t[kind_idx] for t in per_weight)
    return tuple(outs)
```

```python
import functools

import jax
import jax.numpy as jnp
from jax import lax
from jax.experimental import pallas as pl
from jax.experimental.pallas import tpu as pltpu

F32, BF16 = jnp.float32, jnp.bfloat16
MESH = pl.DeviceIdType.MESH
N_DEV = 8
N_LAYERS = 4
N_MOD = 6
HEAD_DIM = 128
POOL_WINDOWS = (2, 4, 8, 16)
EPS = 1e-6
ADAM_LR, ADAM_B1, ADAM_B2, ADAM_EPS, ADAM_WD, ADAM_STEP = 0.001, 0.9, 0.999, 1e-08, 0.01, 10

LANES = 128
HALO = 16
ROW_TILE = 512
CONV_TILE = 256
MM_TILE = 1024
FFN_TILE = 512
ATT_K = 256
ATT_Q = 512
ATT_Q_BWD = 512
ADAM_BLOCK = 256 * 1024
VMEM_LIMIT = 56 * 1024 * 1024

_NN = (((1,), (0,)), ((), ()))
_NT = (((1,), (1,)), ((), ()))
_TN = (((0,), (0,)), ((), ()))
_DIMS = {"nn": _NN, "nt": _NT, "tn": _TN}


def _cp(*sem):
    return pltpu.CompilerParams(dimension_semantics=sem if sem else None, vmem_limit_bytes=VMEM_LIMIT)


def _dot(a, b, dims=_NN):
    return lax.dot_general(a, b, dims, preferred_element_type=F32)


def _ld(ref, nlead):
    return ref[...] if nlead == 0 else ref[(0,) * nlead]


def _st(ref, nlead, val):
    if nlead == 0:
        ref[...] = val
    else:
        ref[(0,) * nlead] = val


def _row_tile(rows, cap, mult=8):
    if rows <= cap:
        return rows
    t = cap - cap % mult
    while rows % t:
        t -= mult
    return t


def _vec_spec(d, nidx):
    zero = (0, 0)
    return pl.BlockSpec((1, d), {1: lambda i: zero, 2: lambda i, j: zero}[nidx])


def _my_index():
    return 4 * lax.axis_index("x") + 2 * lax.axis_index("y") + lax.axis_index("c")


_HBM = pl.BlockSpec(memory_space=pl.ANY)
N_CHIPS = N_DEV // 2


def _remote(src, dst, send_sem, recv_sem, to):
    return pltpu.make_async_remote_copy(src_ref=src, dst_ref=dst, send_sem=send_sem, recv_sem=recv_sem,
                                        device_id=to, device_id_type=MESH)


def _dma_sems(n):
    return [pltpu.SemaphoreType.DMA((n,)), pltpu.SemaphoreType.DMA((n,)), pltpu.SemaphoreType.DMA((1,))]


def all_gather(x, name):
    def body(x_ref, o_ref, send_sems, recv_sems, local_sem):
        x, y, c = lax.axis_index("x"), lax.axis_index("y"), lax.axis_index("c")
        me, sibling = (x, y, c), (x, y, 1 - c)
        chips = [(1 - x, y), (x, 1 - y), (1 - x, 1 - y)]

        def slot(px, py, pc):
            return o_ref.at[4 * px + 2 * py + pc]

        def copy(k, block, to, src=None):
            return _remote(slot(*block) if src is None else src, slot(*block), send_sems.at[k], recv_sems.at[k], to)

        mine = pltpu.make_async_copy(x_ref, slot(*me), local_sem.at[0])
        mine.start()
        first = [copy(0, me, sibling, src=x_ref)]
        first += [copy(1 + j, me, (*chip, c), src=x_ref) for j, chip in enumerate(chips)]
        for cp in first:
            cp.start()
        passed = [copy(4 + j, (*chip, c), sibling) for j, chip in enumerate(chips)]
        for j, chip in enumerate(chips):
            copy(1 + j, (*chip, c), me).wait_recv()
            passed[j].start()
        copy(0, sibling, me).wait_recv()
        for j, chip in enumerate(chips):
            copy(4 + j, (*chip, 1 - c), me).wait_recv()
        for cp in first + passed:
            cp.wait_send()
        mine.wait()

    return pl.pallas_call(
        body, name=name, out_shape=jax.ShapeDtypeStruct((N_DEV,) + x.shape, x.dtype),
        in_specs=[_HBM], out_specs=_HBM, scratch_shapes=_dma_sems(N_DEV - 1),
    )(x)


def sibling_exchange(g8, name):
    def body(x_ref, o_ref, send_sems, recv_sems, _):
        x, y, c = lax.axis_index("x"), lax.axis_index("y"), lax.axis_index("c")
        copies = [_remote(x_ref.at[2 * k + 1 - c], o_ref.at[k], send_sems.at[k], recv_sems.at[k], (x, y, 1 - c))
                  for k in range(N_CHIPS)]
        for cp in copies:
            cp.start()
        for cp in copies:
            cp.wait_recv()
        for cp in copies:
            cp.wait_send()

    return pl.pallas_call(
        body, name=name, out_shape=jax.ShapeDtypeStruct((N_CHIPS,) + g8.shape[1:], g8.dtype),
        in_specs=[_HBM], out_specs=_HBM, scratch_shapes=_dma_sems(N_CHIPS),
    )(g8)


def pair_add(g8, half, name):
    lead = g8.shape[1:]
    C = lead[-1]
    M = 1
    for d in lead[:-1]:
        M *= d
    TR = _row_tile(M, max(16, ADAM_BLOCK // C), 16)

    def body(c_ref, g_ref, h_ref, o_ref):
        o_ref[...] = (g_ref[...].astype(F32) + h_ref[...].astype(F32)).astype(BF16)

    blk = (1, TR, C)
    grid_spec = pltpu.PrefetchScalarGridSpec(
        num_scalar_prefetch=1, grid=(N_CHIPS, M // TR),
        in_specs=[pl.BlockSpec(blk, lambda k, i, c_ref: (2 * k + c_ref[0], i, 0)), pl.BlockSpec(blk, lambda k, i, c_ref: (k, i, 0))],
        out_specs=pl.BlockSpec(blk, lambda k, i, c_ref: (k, i, 0)))
    out = pl.pallas_call(
        body, name=name, grid_spec=grid_spec, out_shape=jax.ShapeDtypeStruct((N_CHIPS, M, C), BF16),
        compiler_params=_cp("parallel", "parallel"),
    )(lax.axis_index("c").astype(jnp.int32).reshape(1), g8.reshape(N_DEV, M, C), half.reshape(N_CHIPS, M, C))
    return out.reshape((N_CHIPS,) + lead)


def chip_exchange(t, name):
    def body(x_ref, o_ref, send_sems, recv_sems, local_sem):
        x, y, c = lax.axis_index("x"), lax.axis_index("y"), lax.axis_index("c")
        mychip = 2 * x + y
        chips = [(1 - x, y), (x, 1 - y), (1 - x, 1 - y)]
        mine = pltpu.make_async_copy(x_ref.at[mychip], o_ref.at[mychip], local_sem.at[0])
        mine.start()
        sends = []
        for j, (px, py) in enumerate(chips):
            cp = _remote(x_ref.at[2 * px + py], o_ref.at[mychip], send_sems.at[j], recv_sems.at[j], (px, py, c))
            cp.start()
            sends.append(cp)
        for j, (px, py) in enumerate(chips):
            _remote(x_ref.at[2 * px + py], o_ref.at[2 * px + py], send_sems.at[j], recv_sems.at[j], (px, py, c)).wait_recv()
        for cp in sends:
            cp.wait_send()
        mine.wait()

    return pl.pallas_call(
        body, name=name, out_shape=jax.ShapeDtypeStruct(t.shape, t.dtype),
        in_specs=[_HBM], out_specs=_HBM, scratch_shapes=_dma_sems(N_CHIPS - 1),
    )(t)


def reduce_exchange(g8, name):
    half = sibling_exchange(g8, "d2d_" + name)
    return chip_exchange(pair_add(g8, half, "pair_" + name), "ici_" + name)


_CARRY_COPIES = {"spread": 4, "chip": 3}


def _carry_shape(kind, x):
    return jax.ShapeDtypeStruct(((N_DEV,) + x.shape) if kind == "spread" else x.shape, x.dtype)


def _carry_scratch(kind, n):
    return [pltpu.SemaphoreType.DMA((n * _CARRY_COPIES[kind],)), pltpu.SemaphoreType.DMA((n * _CARRY_COPIES[kind],)),
            pltpu.SemaphoreType.DMA((n,))]


def _carry_copies(kind, x_refs, o_refs, send_sems, recv_sems, local_sems):
    x, y, c = lax.axis_index("x"), lax.axis_index("y"), lax.axis_index("c")
    chips = [(1 - x, y), (x, 1 - y), (1 - x, 1 - y)]
    sends, recvs, local = [], [], []
    for a, (x_ref, o_ref) in enumerate(zip(x_refs, o_refs)):
        base = a * _CARRY_COPIES[kind]
        if kind == "spread":
            me = 4 * x + 2 * y + c
            peers = [(x, y, 1 - c)] + [(px, py, c) for px, py in chips]
            src = [x_ref] * 4
            mine = [me] * 4
            theirs = [4 * px + 2 * py + pc for px, py, pc in peers]
            local.append(pltpu.make_async_copy(x_ref, o_ref.at[me], local_sems.at[a]))
        else:
            mychip = 2 * x + y
            peers = [(px, py, c) for px, py in chips]
            theirs = [2 * px + py for px, py in chips]
            src = [x_ref.at[t] for t in theirs]
            mine = [mychip] * 3
            local.append(pltpu.make_async_copy(x_ref.at[mychip], o_ref.at[mychip], local_sems.at[a]))
        for k, peer in enumerate(peers):
            sends.append(_remote(src[k], o_ref.at[mine[k]], send_sems.at[base + k], recv_sems.at[base + k], peer))
            recvs.append(_remote(src[k], o_ref.at[theirs[k]], send_sems.at[base + k], recv_sems.at[base + k], peer))
    return sends, recvs, local


def _carry_run(kind, x_refs, o_refs, sems, first, last):
    sends, recvs, local = _carry_copies(kind, x_refs, o_refs, *sems)

    @pl.when(first)
    def _():
        for cp in local + sends:
            cp.start()

    @pl.when(last)
    def _():
        for cp in recvs:
            cp.wait_recv()
        for cp in sends:
            cp.wait_send()
        for cp in local:
            cp.wait()


def gather_pass(o8, name):
    def body(i_ref, o_ref, send_sems, recv_sems, _):
        del i_ref
        x, y, c = lax.axis_index("x"), lax.axis_index("y"), lax.axis_index("c")
        chips = [(1 - x, y), (x, 1 - y), (1 - x, 1 - y)]
        sends = [_remote(o_ref.at[4 * px + 2 * py + c], o_ref.at[4 * px + 2 * py + c], send_sems.at[j], recv_sems.at[j],
                         (x, y, 1 - c)) for j, (px, py) in enumerate(chips)]
        for cp in sends:
            cp.start()
        for j, (px, py) in enumerate(chips):
            slot = o_ref.at[4 * px + 2 * py + 1 - c]
            _remote(slot, slot, send_sems.at[j], recv_sems.at[j], (x, y, 1 - c)).wait_recv()
        for cp in sends:
            cp.wait_send()

    return pl.pallas_call(
        body, name=name, out_shape=jax.ShapeDtypeStruct(o8.shape, o8.dtype), in_specs=[_HBM], out_specs=_HBM,
        scratch_shapes=_dma_sems(N_CHIPS - 1), input_output_aliases={0: 0},
    )(o8)


def mm(a, b, *, mode, grid, a_blk, a_map, b_blk, b_map, o_shape, o_blk, o_map, o_dtype, name, carry=None):
    nk = grid[2]
    na, nb, no = len(a_blk) - 2, len(b_blk) - 2, len(o_blk) - 2
    dims = _DIMS[mode]
    kind, carried = carry if carry is not None else (None, [])
    nc = len(carried)

    def body(*refs):
        a_ref, b_ref = refs[:2]
        o_ref = refs[2 + nc]
        scratch = refs[3 + 2 * nc:]
        if nc:
            pos = [pl.program_id(d) for d in range(3)]
            first = (pos[0] == 0) & (pos[1] == 0) & (pos[2] == 0)
            last = (pos[0] == grid[0] - 1) & (pos[1] == grid[1] - 1) & (pos[2] == grid[2] - 1)
            _carry_run(kind, refs[2:2 + nc], refs[3 + nc:3 + 2 * nc], scratch[-3:], first, last)
        p = _dot(_ld(a_ref, na), _ld(b_ref, nb), dims)
        if nk == 1:
            _st(o_ref, no, p.astype(o_dtype))
        else:
            acc = scratch[0]
            k = pl.program_id(2)

            @pl.when(k == 0)
            def _():
                acc[...] = p

            @pl.when(k > 0)
            def _():
                acc[...] += p

            @pl.when(k == nk - 1)
            def _():
                _st(o_ref, no, acc[...].astype(o_dtype))

    scratch = [pltpu.VMEM(tuple(o_blk[-2:]), F32)] if nk > 1 else []
    out = pl.pallas_call(
        body, name=name, grid=grid,
        out_shape=[jax.ShapeDtypeStruct(o_shape, o_dtype)] + [_carry_shape(kind, x) for x in carried],
        in_specs=[pl.BlockSpec(a_blk, a_map), pl.BlockSpec(b_blk, b_map)] + [_HBM] * nc,
        out_specs=[pl.BlockSpec(o_blk, o_map)] + [_HBM] * nc,
        scratch_shapes=scratch + (_carry_scratch(kind, nc) if nc else []),
        compiler_params=_cp(*((("arbitrary",) * 3) if nc else ("parallel", "parallel", "arbitrary"))),
    )(a, b, *carried)
    return out if nc else out[0]


def proj_fwd(u, w8, o_dtype, name):
    S, D = u.shape
    nc = w8.shape[-1]
    T = min(MM_TILE, S)
    return mm(u, w8, mode="nn", grid=(S // T, N_DEV, 1), a_blk=(T, D), a_map=lambda i, j, k: (i, 0),
              b_blk=(1, 1, D, nc), b_map=lambda i, j, k: (j, 0, 0, 0), o_shape=(S, N_DEV * nc), o_blk=(T, nc),
              o_map=lambda i, j, k: (i, j), o_dtype=o_dtype, name=name)


def proj_bwd_x(dy, w8, name):
    S = dy.shape[0]
    D, nc = w8.shape[-2:]
    T = min(MM_TILE, S)
    return mm(dy, w8, mode="nt", grid=(S // T, 1, N_DEV), a_blk=(T, nc), a_map=lambda i, j, k: (i, k),
              b_blk=(1, 1, D, nc), b_map=lambda i, j, k: (k, 0, 0, 0), o_shape=(S, D), o_blk=(T, D),
              o_map=lambda i, j, k: (i, 0), o_dtype=F32, name=name)


def proj_bwd_w(u, dy, name):
    S, D = u.shape
    nc = dy.shape[1] // N_DEV
    T = min(MM_TILE, S)
    return mm(u, dy, mode="tn", grid=(N_DEV, 1, S // T), a_blk=(T, D), a_map=lambda j, _, k: (k, 0),
              b_blk=(T, nc), b_map=lambda j, _, k: (k, j), o_shape=(N_DEV, 1, D, nc), o_blk=(1, 1, D, nc),
              o_map=lambda j, _, k: (j, 0, 0, 0), o_dtype=BF16, name=name)


def dense_fwd(a, w, name):
    S, D = a.shape
    T = min(MM_TILE, S)
    return mm(a, w, mode="nn", grid=(S // T, 1, 1), a_blk=(T, D), a_map=lambda i, j, k: (i, 0), b_blk=(D, D),
              b_map=lambda i, j, k: (0, 0), o_shape=(S, D), o_blk=(T, D), o_map=lambda i, j, k: (i, 0),
              o_dtype=BF16, name=name)


def dense_bwd_x(dy, w, name):
    S, D = dy.shape
    T = min(MM_TILE, S)
    return mm(dy, w, mode="nt", grid=(S // T, 1, 1), a_blk=(T, D), a_map=lambda i, j, k: (i, 0), b_blk=(D, D),
              b_map=lambda i, j, k: (0, 0), o_shape=(S, D), o_blk=(T, D), o_map=lambda i, j, k: (i, 0),
              o_dtype=BF16, name=name)


def dense_bwd_w(a, dy, name):
    S, D = a.shape
    T = min(MM_TILE, S)
    half = D // 2
    return mm(a, dy, mode="tn", grid=(1, 2, S // T), a_blk=(T, D), a_map=lambda i, j, k: (k, 0), b_blk=(T, half),
              b_map=lambda i, j, k: (k, j), o_shape=(D, D), o_blk=(D, half), o_map=lambda i, j, k: (0, j),
              o_dtype=BF16, name=name)


def _mod_cols(nloc):
    return 256 if nloc % 256 == 0 else nloc


def mod_project(c_pad, w_mod, name):
    L, D, nloc = w_mod.shape
    R = c_pad.shape[0]
    tn = _mod_cols(nloc)

    def body(c_ref, w_ref, o_ref):
        c = c_ref[...]
        s = (c / (1.0 + jnp.exp(-c))).astype(BF16)
        o_ref[0] = _dot(s, w_ref[0].astype(BF16))

    return pl.pallas_call(
        body, name=name, grid=(L, nloc // tn), out_shape=jax.ShapeDtypeStruct((L, R, nloc), F32),
        in_specs=[pl.BlockSpec((R, D), lambda l, j: (0, 0)), pl.BlockSpec((1, D, tn), lambda l, j: (l, 0, j))],
        out_specs=pl.BlockSpec((1, R, tn), lambda l, j: (l, 0, j)), compiler_params=_cp("parallel", "parallel"),
    )(c_pad, w_mod)


def _adam(w, g, m, v):
    m = ADAM_B1 * m + (1.0 - ADAM_B1) * g
    v = ADAM_B2 * v + (1.0 - ADAM_B2) * (g * g)
    m_hat = m / (1.0 - ADAM_B1 ** ADAM_STEP)
    v_hat = v / (1.0 - ADAM_B2 ** ADAM_STEP)
    delta = -ADAM_LR * (m_hat / (jnp.sqrt(v_hat) + ADAM_EPS) + ADAM_WD * w)
    return delta, m, v


def mod_grad_adam(c_pad, dmod_pad, w, m, v, name):
    L, D, nloc = w.shape
    R = c_pad.shape[0]
    tn = _mod_cols(nloc)

    def body(c_ref, d_ref, w_ref, m_ref, v_ref, g_ref, dl_ref, nm_ref, nv_ref):
        c = c_ref[...]
        s = (c / (1.0 + jnp.exp(-c))).astype(BF16)
        g = _dot(s, d_ref[0].astype(BF16), _TN)
        delta, nm, nv = _adam(w_ref[0], g, m_ref[0], v_ref[0])
        g_ref[0], dl_ref[0], nm_ref[0], nv_ref[0] = g, delta, nm, nv

    wspec = pl.BlockSpec((1, D, tn), lambda l, j: (l, 0, j))
    out = jax.ShapeDtypeStruct(w.shape, F32)
    return pl.pallas_call(
        body, name=name, grid=(L, nloc // tn), out_shape=(out,) * 4,
        in_specs=[pl.BlockSpec((R, D), lambda l, j: (0, 0)), pl.BlockSpec((1, R, tn), lambda l, j: (l, 0, j)),
                  wspec, wspec, wspec],
        out_specs=(wspec,) * 4, compiler_params=_cp("parallel", "parallel"),
    )(c_pad, dmod_pad, w, m, v)


def resid_norm(h_prev, y_prev, gate_prev, gn, sc, sh, u_dtype, name):
    S, D = h_prev.shape
    T = min(ROW_TILE, S)
    has_res = y_prev is not None

    def body(*refs):
        if has_res:
            h_ref, y_ref, gate_ref, gn_ref, sc_ref, sh_ref, ho_ref, u_ref = refs
            h = h_ref[...] + gate_ref[...] * y_ref[...].astype(F32)
            ho_ref[...] = h
        else:
            h_ref, gn_ref, sc_ref, sh_ref, u_ref = refs
            h = h_ref[...]
        r = lax.rsqrt(jnp.mean(h * h, axis=-1, keepdims=True) + EPS)
        u = (h * r) * gn_ref[...] * (1.0 + sc_ref[...]) + sh_ref[...]
        u_ref[...] = u.astype(u_dtype)

    row = pl.BlockSpec((T, D), lambda i: (i, 0))
    vec = _vec_spec(D, 1)
    if has_res:
        h, u = pl.pallas_call(
            body, name=name, grid=(S // T,), out_shape=(jax.ShapeDtypeStruct((S, D), F32), jax.ShapeDtypeStruct((S, D), u_dtype)),
            in_specs=[row, row, vec, vec, vec, vec], out_specs=(row, row), compiler_params=_cp("parallel"),
        )(h_prev, y_prev, gate_prev, gn, sc, sh)
        return h, u
    u = pl.pallas_call(
        body, name=name, grid=(S // T,), out_shape=jax.ShapeDtypeStruct((S, D), u_dtype),
        in_specs=[row, vec, vec, vec], out_specs=row, compiler_params=_cp("parallel"),
    )(h_prev, gn, sc, sh)
    return h_prev, u


def gate_bwd(dh, y, gate, name):
    S, D = dh.shape
    T = min(ROW_TILE, S)

    def body(dh_ref, y_ref, gate_ref, dy_ref, dg_ref):
        i = pl.program_id(0)
        d = dh_ref[...]
        dy_ref[...] = (d * gate_ref[...]).astype(BF16)
        part = jnp.sum(d * y_ref[...].astype(F32), axis=0, keepdims=True)

        @pl.when(i == 0)
        def _():
            dg_ref[...] = part

        @pl.when(i > 0)
        def _():
            dg_ref[...] += part

    row = pl.BlockSpec((T, D), lambda i: (i, 0))
    vec = _vec_spec(D, 1)
    return pl.pallas_call(
        body, name=name, grid=(S // T,), out_shape=(jax.ShapeDtypeStruct((S, D), BF16), jax.ShapeDtypeStruct((1, D), F32)),
        in_specs=[row, row, vec], out_specs=(row, vec), compiler_params=_cp("arbitrary"),
    )(dh, y, gate)


def norm_bwd(du, h, dh_res, gn, sc, name):
    S, D = h.shape
    T = min(ROW_TILE, S)

    def body(du_ref, h_ref, dr_ref, gn_ref, sc_ref, dh_ref, dgn_ref, dsc_ref, dsh_ref):
        i = pl.program_id(0)
        d = du_ref[...].astype(F32)
        hh = h_ref[...]
        r = lax.rsqrt(jnp.mean(hh * hh, axis=-1, keepdims=True) + EPS)
        n = hh * r
        gn_v = gn_ref[...]
        dng = d * (1.0 + sc_ref[...])
        dn = dng * gn_v
        dh_ref[...] = dr_ref[...] + r * (dn - n * jnp.mean(dn * n, axis=-1, keepdims=True))
        parts = (jnp.sum(dng * n, axis=0, keepdims=True), jnp.sum(d * (n * gn_v), axis=0, keepdims=True),
                 jnp.sum(d, axis=0, keepdims=True))

        @pl.when(i == 0)
        def _():
            for ref, part in zip((dgn_ref, dsc_ref, dsh_ref), parts):
                ref[...] = part

        @pl.when(i > 0)
        def _():
            for ref, part in zip((dgn_ref, dsc_ref, dsh_ref), parts):
                ref[...] += part

    row = pl.BlockSpec((T, D), lambda i: (i, 0))
    vec = _vec_spec(D, 1)
    vshape = jax.ShapeDtypeStruct((1, D), F32)
    return pl.pallas_call(
        body, name=name, grid=(S // T,), out_shape=(jax.ShapeDtypeStruct((S, D), F32), vshape, vshape, vshape),
        in_specs=[row, row, row, vec, vec], out_specs=(row, vec, vec, vec), compiler_params=_cp("arbitrary"),
    )(du, h, dh_res, gn, sc)


def final_loss(h_prev, y_prev, gate_prev, fg, target, name):
    S, D = h_prev.shape
    T = min(ROW_TILE, S)

    def body(h_ref, y_ref, gate_ref, fg_ref, t_ref, dh_ref, dfg_ref, loss_ref):
        i = pl.program_id(0)
        h = h_ref[...] + gate_ref[...] * y_ref[...].astype(F32)
        r = lax.rsqrt(jnp.mean(h * h, axis=-1, keepdims=True) + EPS)
        n = h * r
        g = fg_ref[...]
        err = n * g - t_ref[...]
        dout = err * (1.0 / D)
        dn = dout * g
        dh_ref[...] = r * (dn - n * jnp.mean(dn * n, axis=-1, keepdims=True))
        dfg = jnp.sum(dout * n, axis=0, keepdims=True)
        part = 0.5 * jnp.sum(jnp.mean(err * err, axis=-1, keepdims=True), axis=0, keepdims=True)
        part = jnp.broadcast_to(part, (1, LANES))

        @pl.when(i == 0)
        def _():
            dfg_ref[...] = dfg
            loss_ref[...] = part

        @pl.when(i > 0)
        def _():
            dfg_ref[...] += dfg
            loss_ref[...] += part

    row = pl.BlockSpec((T, D), lambda i: (i, 0))
    vec = _vec_spec(D, 1)
    return pl.pallas_call(
        body, name=name, grid=(S // T,),
        out_shape=(jax.ShapeDtypeStruct((S, D), F32), jax.ShapeDtypeStruct((1, D), F32), jax.ShapeDtypeStruct((1, LANES), F32)),
        in_specs=[row, row, vec, vec, row], out_specs=(row, vec, pl.BlockSpec((1, LANES), lambda i: (0, 0))),
        compiler_params=_cp("arbitrary"),
    )(h_prev, y_prev, gate_prev, fg, target)


def _prev_halo(T):
    return lambda i: (jnp.maximum(i * (T // HALO) - 1, 0), 0)


def _next_halo(T, S):
    return lambda i: (jnp.minimum((i + 1) * (T // HALO), S // HALO - 1), 0)


def pool_fwd(u, w, ps, name):
    S, D = u.shape
    T = min(ROW_TILE, S)
    C = D // len(POOL_WINDOWS)

    def body(uc_ref, up_ref, w_ref, ps_ref, diff_ref, y_ref):
        i = pl.program_id(0)
        t = lax.broadcasted_iota(jnp.int32, (T, 1), 0) + i * T
        for g, win in enumerate(POOL_WINDOWS):
            cols = slice(g * C, (g + 1) * C)
            cur = uc_ref[:, cols]
            prev = jnp.where(i > 0, up_ref[:, cols], 0.0)
            s = jnp.concatenate([prev, cur], axis=0)
            k = 1
            while k < win:
                s = s + pltpu.roll(s, k, 0)
                k *= 2
            cnt = jnp.minimum(t + 1, win).astype(F32)
            diff = (s[HALO:, :] / cnt - cur).astype(BF16)
            diff_ref[:, cols] = diff
            y_ref[:, cols] = (_dot(diff, w_ref[g]) * ps_ref[:, cols]).astype(BF16)

    row = pl.BlockSpec((T, D), lambda i: (i, 0))
    out = jax.ShapeDtypeStruct((S, D), BF16)
    return pl.pallas_call(
        body, name=name, grid=(S // T,), out_shape=(out, out),
        in_specs=[row, pl.BlockSpec((HALO, D), _prev_halo(T)), pl.BlockSpec(w.shape, lambda i: (0, 0, 0)), _vec_spec(D, 1)],
        out_specs=(row, row), compiler_params=_cp("parallel"),
    )(u, u, w, ps)


def pool_bwd(dy, diff, w, ps, name):
    S, D = dy.shape
    T = min(ROW_TILE, S)
    G = len(POOL_WINDOWS)
    C = D // G

    def body(dy_ref, diff_ref, w_ref, ps_ref, dd_ref, dw_ref, dps_ref):
        i = pl.program_id(0)
        for g in range(G):
            cols = slice(g * C, (g + 1) * C)
            diff = diff_ref[:, cols]
            d = dy_ref[:, cols].astype(F32)
            ypre = _dot(diff, w_ref[g])
            dps = jnp.sum(d * ypre, axis=0, keepdims=True)
            dyp = (d * ps_ref[:, cols]).astype(BF16)
            dd_ref[:, cols] = _dot(dyp, w_ref[g], _NT)
            dw = _dot(diff, dyp, _TN)

            @pl.when(i == 0)
            def _():
                dw_ref[g] = dw
                dps_ref[:, cols] = dps

            @pl.when(i > 0)
            def _():
                dw_ref[g] += dw
                dps_ref[:, cols] += dps

    row = pl.BlockSpec((T, D), lambda i: (i, 0))
    wspec = pl.BlockSpec(w.shape, lambda i: (0, 0, 0))
    return pl.pallas_call(
        body, name=name, grid=(S // T,),
        out_shape=(jax.ShapeDtypeStruct((S, D), F32), jax.ShapeDtypeStruct(w.shape, F32), jax.ShapeDtypeStruct((1, D), F32)),
        in_specs=[row, row, wspec, _vec_spec(D, 1)], out_specs=(row, wspec, _vec_spec(D, 1)),
        compiler_params=_cp("arbitrary"),
    )(dy, diff, w, ps)


def pool_window_bwd(dd, name):
    S, D = dd.shape
    T = min(ROW_TILE, S)
    C = D // len(POOL_WINDOWS)
    n = T + HALO
    last = S // T - 1

    def body(dc_ref, dn_ref, du_ref):
        i = pl.program_id(0)
        t = lax.broadcasted_iota(jnp.int32, (n, 1), 0) + i * T
        for g, win in enumerate(POOL_WINDOWS):
            cols = slice(g * C, (g + 1) * C)
            cur = dc_ref[:, cols]
            nxt = jnp.where(i < last, dn_ref[:, cols], 0.0)
            cnt = jnp.minimum(t + 1, win).astype(F32)
            s = jnp.concatenate([cur, nxt], axis=0) / cnt
            k = 1
            while k < win:
                s = s + pltpu.roll(s, n - k, 0)
                k *= 2
            du_ref[:, cols] = s[:T, :] - cur

    row = pl.BlockSpec((T, D), lambda i: (i, 0))
    return pl.pallas_call(
        body, name=name, grid=(S // T,), out_shape=jax.ShapeDtypeStruct((S, D), F32),
        in_specs=[row, pl.BlockSpec((HALO, D), _next_halo(T, S))], out_specs=row, compiler_params=_cp("parallel"),
    )(dd, dd)


def _conv_chunk(D):
    return 512 if D % 512 == 0 else D


def conv_fwd(U, cw, name):
    S = U.shape[0]
    D = U.shape[1] // 3
    T = min(CONV_TILE, S)
    CH = _conv_chunk(D)

    def body(uc_ref, up_ref, cw_ref, q_ref):
        i = pl.program_id(0)
        for j in range(D // CH):
            cols = slice(j * CH, (j + 1) * CH)
            ccols = slice(D + j * CH, D + (j + 1) * CH)
            vcols = slice(2 * D + j * CH, 2 * D + (j + 1) * CH)
            zp = jnp.where(i > 0, up_ref[:, ccols] * up_ref[:, vcols], 0.0)
            z = jnp.concatenate([zp, uc_ref[:, ccols] * uc_ref[:, vcols]], axis=0)
            zc = cw_ref[2, :, cols] * z + cw_ref[1, :, cols] * pltpu.roll(z, 1, 0) + cw_ref[0, :, cols] * pltpu.roll(z, 2, 0)
            q_ref[:, cols] = (uc_ref[:, cols] * zc[HALO:, :]).astype(BF16)

    return pl.pallas_call(
        body, name=name, grid=(S // T,), out_shape=jax.ShapeDtypeStruct((S, D), BF16),
        in_specs=[pl.BlockSpec((T, 3 * D), lambda i: (i, 0)), pl.BlockSpec((HALO, 3 * D), _prev_halo(T)),
                  pl.BlockSpec((3, 1, D), lambda i: (0, 0, 0))],
        out_specs=pl.BlockSpec((T, D), lambda i: (i, 0)), compiler_params=_cp("parallel"),
    )(U, U, cw)


def conv_bwd(dq, U, cw, name):
    S = U.shape[0]
    D = U.shape[1] // 3
    T = min(CONV_TILE, S)
    CH = _conv_chunk(D)
    n = T + HALO
    last = S // T - 1

    def body(dq_ref, dqn_ref, uc_ref, up_ref, un_ref, cw_ref, du_ref, dcw_ref):
        i = pl.program_id(0)
        for j in range(D // CH):
            cols = slice(j * CH, (j + 1) * CH)
            ccols = slice(D + j * CH, D + (j + 1) * CH)
            vcols = slice(2 * D + j * CH, 2 * D + (j + 1) * CH)
            w0, w1, w2 = cw_ref[0, :, cols], cw_ref[1, :, cols], cw_ref[2, :, cols]
            b, c, v = uc_ref[:, cols], uc_ref[:, ccols], uc_ref[:, vcols]
            dqc = dq_ref[:, cols].astype(F32)
            zp = jnp.where(i > 0, up_ref[:, ccols] * up_ref[:, vcols], 0.0)
            z = jnp.concatenate([zp, c * v], axis=0)
            z1 = pltpu.roll(z, 1, 0)[HALO:, :]
            z2 = pltpu.roll(z, 2, 0)[HALO:, :]
            z0 = z[HALO:, :]
            zc = w2 * z0 + w1 * z1 + w0 * z2
            dzc = dqc * b
            dzn = jnp.where(i < last, dqn_ref[:, cols].astype(F32) * un_ref[:, cols], 0.0)
            e = jnp.concatenate([dzc, dzn], axis=0)
            dz = (w2 * e + w1 * pltpu.roll(e, n - 1, 0) + w0 * pltpu.roll(e, n - 2, 0))[:T, :]
            du_ref[:, cols] = (dqc * zc).astype(BF16)
            du_ref[:, ccols] = (dz * v).astype(BF16)
            du_ref[:, vcols] = (dz * c).astype(BF16)
            parts = [jnp.sum(dzc * zz, axis=0, keepdims=True) for zz in (z2, z1, z0)]

            @pl.when(i == 0)
            def _():
                for k in range(3):
                    dcw_ref[k, :, cols] = parts[k]

            @pl.when(i > 0)
            def _():
                for k in range(3):
                    dcw_ref[k, :, cols] += parts[k]

    return pl.pallas_call(
        body, name=name, grid=(S // T,),
        out_shape=(jax.ShapeDtypeStruct((S, 3 * D), BF16), jax.ShapeDtypeStruct((3, 1, D), F32)),
        in_specs=[pl.BlockSpec((T, D), lambda i: (i, 0)), pl.BlockSpec((HALO, D), _next_halo(T, S)),
                  pl.BlockSpec((T, 3 * D), lambda i: (i, 0)), pl.BlockSpec((HALO, 3 * D), _prev_halo(T)),
                  pl.BlockSpec((HALO, 3 * D), _next_halo(T, S)), pl.BlockSpec((3, 1, D), lambda i: (0, 0, 0))],
        out_specs=(pl.BlockSpec((T, 3 * D), lambda i: (i, 0)), pl.BlockSpec((3, 1, D), lambda i: (0, 0, 0))),
        compiler_params=_cp("arbitrary"),
    )(dq, dq, U, U, U, cw)


MASKED_LOG = -1e30
UNDERFLOW_LOG = -105.0


def _sb_logits(z, q0, k0, masked):
    z = z * (HEAD_DIM ** -0.5)
    lb = jnp.minimum(z, 0.0) - jnp.log(1.0 + jnp.exp(-jnp.abs(z)))
    l1 = lb - z
    if masked:
        causal = (lax.broadcasted_iota(jnp.int32, z.shape, 1) + k0) < (lax.broadcasted_iota(jnp.int32, z.shape, 0) + q0)
        lb = jnp.where(causal, lb, MASKED_LOG)
        l1 = jnp.where(causal, l1, 0.0)
    return lb, l1


def _key_sums(l1, tri):
    hi = l1.astype(BF16)
    lo = (l1 - hi.astype(F32)).astype(BF16)
    return _dot(hi, tri) + _dot(lo, tri)


def _tri(B, cmp):
    row = lax.broadcasted_iota(jnp.int32, (B, B), 0)
    col = lax.broadcasted_iota(jnp.int32, (B, B), 1)
    return cmp(row, col).astype(BF16)


def attn_fwd(qkv, name):
    S = qkv.shape[0]
    D = qkv.shape[1] // 3
    H = D // HEAD_DIM
    BK = min(ATT_K, S)
    BQ = min(ATT_Q, S)
    ND = BQ // BK
    R = BQ // min(ATT_Q_BWD, S)

    def body(q_ref, k_ref, v_ref, o_ref, lt_ref, first_ref):
        later = _tri(BK, lambda j, s: j > s)
        h = pl.program_id(0)

        def q_loop(qi, _):
            q0 = pl.multiple_of(qi * BQ, BQ)
            qb = q_ref[pl.ds(q0, BQ), :]
            n_off = qi * ND

            def scores(kb):
                k0 = pl.multiple_of(kb * BK, BK)
                return _dot(qb, k_ref[pl.ds(k0, BK), :], _NT)

            def accumulate(kb, z, masked, carry, oacc):
                k0 = pl.multiple_of(kb * BK, BK)
                lb, l1 = _sb_logits(z, q0, k0, masked)
                suffix = _key_sums(l1, later) + carry
                a = jnp.exp(lb + suffix)
                oacc = oacc + _dot(a.astype(BF16), v_ref[pl.ds(k0, BK), :])
                return carry + jnp.sum(l1, axis=1, keepdims=True), oacc

            carry, oacc = jnp.zeros((BQ, 1), F32), jnp.zeros((BQ, HEAD_DIM), F32)
            z = scores(n_off + ND - 1)
            for d in reversed(range(ND)):
                z_next = scores(jnp.maximum(n_off + d - 1, 0))
                carry, oacc = accumulate(n_off + d, z, True, carry, oacc)
                z = z_next

            def live(carry):
                return (jnp.max(carry) >= UNDERFLOW_LOG).astype(jnp.int32)

            def k_cond(state):
                t, go = state[0], state[1]
                return (t < n_off) & (go > 0)

            def k_body(state):
                t, _, carry, oacc, z = state
                kb = n_off - 1 - t
                z_next = scores(jnp.maximum(kb - 1, 0))
                carry, oacc = accumulate(kb, z, False, carry, oacc)
                return t + 1, live(carry), carry, oacc, z_next

            visited, _, total, oacc, _ = lax.while_loop(k_cond, k_body, (jnp.int32(0), live(carry), carry, oacc, z))
            o_ref[pl.ds(q0, BQ), :] = oacc.astype(BF16)
            lt_ref[pl.ds(q0, BQ), :] = jnp.broadcast_to(total, (BQ, HEAD_DIM))
            for r in range(R):
                first_ref[h, qi * R + r] = n_off - visited
            return 0

        lax.fori_loop(0, S // BQ, q_loop, 0)

    def head(off):
        return pl.BlockSpec((S, HEAD_DIM), lambda h: (0, off + h))

    return pl.pallas_call(
        body, name=name, grid=(H,),
        out_shape=(jax.ShapeDtypeStruct((S, D), BF16), jax.ShapeDtypeStruct((S, D), F32),
                   jax.ShapeDtypeStruct((H, S // min(ATT_Q_BWD, S)), jnp.int32)),
        in_specs=[head(0), head(H), head(2 * H)],
        out_specs=(head(0), head(0), pl.BlockSpec(memory_space=pltpu.SMEM)), compiler_params=_cp("arbitrary"),
    )(qkv, qkv, qkv)


def attn_bwd(qkv, lt, first, do, name):
    S = qkv.shape[0]
    D = qkv.shape[1] // 3
    H = D // HEAD_DIM
    BK = min(ATT_K, S)
    BQ = min(ATT_Q_BWD, S)
    ND = BQ // BK
    scale = HEAD_DIM ** -0.5

    def body(q_ref, k_ref, v_ref, lt_ref, first_ref, do_ref, dq_ref, dk_ref, dv_ref, dk_acc, dv_acc):
        upto = _tri(BK, lambda j, s: j <= s)
        before = _tri(BK, lambda j, s: j < s)
        dk_acc[...] = jnp.zeros_like(dk_acc)
        dv_acc[...] = jnp.zeros_like(dv_acc)
        h = pl.program_id(0)

        def q_loop(qi, _):
            q0 = pl.multiple_of(qi * BQ, BQ)
            qb = q_ref[pl.ds(q0, BQ), :]
            dob = do_ref[pl.ds(q0, BQ), :]
            total = lt_ref[pl.ds(q0, BQ), :][:, :1]
            n_off = qi * ND
            first = first_ref[h, qi]

            def scores(kb):
                k0 = pl.multiple_of(kb * BK, BK)
                return _dot(qb, k_ref[pl.ds(k0, BK), :], _NT), _dot(dob, v_ref[pl.ds(k0, BK), :], _NT)

            def accumulate(kb, z, da, masked, lsum, gsum, dqacc):
                k0 = pl.multiple_of(kb * BK, BK)
                lb, l1 = _sb_logits(z, q0, k0, masked)
                suffix = total - lsum - _key_sums(l1, upto)
                a = jnp.exp(lb + suffix)
                g = a * da
                gpre = gsum + _dot(g.astype(BF16), before)
                beta = jnp.exp(lb)
                dz = ((g * (1.0 - beta) - gpre * beta) * scale).astype(BF16)
                dqacc = dqacc + _dot(dz, k_ref[pl.ds(k0, BK), :])
                dk_acc[pl.ds(k0, BK), :] += _dot(dz, qb, _TN)
                dv_acc[pl.ds(k0, BK), :] += _dot(a.astype(BF16), dob, _TN)
                return (lsum + jnp.sum(l1, axis=1, keepdims=True), gsum + jnp.sum(g, axis=1, keepdims=True), dqacc)

            def k_loop(t, state):
                lsum, gsum, dqacc, z, da = state
                z_next, da_next = scores(t + 1)
                lsum, gsum, dqacc = accumulate(t, z, da, False, lsum, gsum, dqacc)
                return lsum, gsum, dqacc, z_next, da_next

            zero = jnp.zeros((BQ, 1), F32)
            lsum, gsum, dqacc, z, da = lax.fori_loop(
                first, n_off, k_loop, (zero, zero, jnp.zeros((BQ, HEAD_DIM), F32), *scores(first)))
            for d in range(ND):
                nxt = scores(n_off + d + 1) if d + 1 < ND else None
                lsum, gsum, dqacc = accumulate(n_off + d, z, da, True, lsum, gsum, dqacc)
                if nxt is not None:
                    z, da = nxt
            dq_ref[pl.ds(q0, BQ), :] = dqacc.astype(BF16)
            return 0

        lax.fori_loop(0, S // BQ, q_loop, 0)
        dk_ref[...] = dk_acc[...].astype(BF16)
        dv_ref[...] = dv_acc[...].astype(BF16)

    def head(off):
        return pl.BlockSpec((S, HEAD_DIM), lambda h: (0, off + h))

    out = jax.ShapeDtypeStruct((S, D), BF16)
    return pl.pallas_call(
        body, name=name, grid=(H,), out_shape=(out, out, out),
        in_specs=[head(0), head(H), head(2 * H), head(0), pl.BlockSpec(memory_space=pltpu.SMEM), head(0)],
        out_specs=(head(0), head(0), head(0)),
        scratch_shapes=[pltpu.VMEM((S, HEAD_DIM), F32), pltpu.VMEM((S, HEAD_DIM), F32)], compiler_params=_cp("parallel"),
    )(qkv, qkv, qkv, lt, first, do)


def ffn_fwd(u, wgu8, wd8, name, spread=()):
    S, D = u.shape
    n = wgu8.shape[-1]
    T = min(FFN_TILE, S)
    nc = len(spread)
    nsteps = S // T

    def body(*refs):
        u_ref, wgu_ref, wd_ref = refs[:3]
        gp_ref, y_ref = refs[3 + nc:5 + nc]
        acc = refs[5 + 2 * nc]
        i, j = pl.program_id(0), pl.program_id(1)
        if nc:
            _carry_run("spread", refs[3:3 + nc], refs[5 + nc:5 + 2 * nc], refs[6 + 2 * nc:],
                       (i == 0) & (j == 0), (i == nsteps - 1) & (j == N_DEV - 1))
        uu = u_ref[...]
        g = _dot(uu, wgu_ref[0, 0])
        p = _dot(uu, wgu_ref[0, 1])
        gp_ref[0, 0] = g.astype(BF16)
        gp_ref[1, 0] = p.astype(BF16)
        a = (g / (1.0 + jnp.exp(-g)) * p).astype(BF16)
        part = _dot(a, wd_ref[0])

        @pl.when(j == 0)
        def _():
            acc[...] = part

        @pl.when(j > 0)
        def _():
            acc[...] += part

        @pl.when(j == N_DEV - 1)
        def _():
            y_ref[...] = acc[...].astype(BF16)

    return pl.pallas_call(
        body, name=name, grid=(nsteps, N_DEV),
        out_shape=[jax.ShapeDtypeStruct((2, N_DEV, S, n), BF16), jax.ShapeDtypeStruct((S, D), BF16)]
        + [_carry_shape("spread", x) for x in spread],
        in_specs=[pl.BlockSpec((T, D), lambda i, j: (i, 0)), pl.BlockSpec((1, 2, D, n), lambda i, j: (j, 0, 0, 0)),
                  pl.BlockSpec((1, n, D), lambda i, j: (j, 0, 0))] + [_HBM] * nc,
        out_specs=[pl.BlockSpec((2, 1, T, n), lambda i, j: (0, j, i, 0)), pl.BlockSpec((T, D), lambda i, j: (i, 0))]
        + [_HBM] * nc,
        scratch_shapes=[pltpu.VMEM((T, D), F32)] + (_carry_scratch("spread", nc) if nc else []),
        compiler_params=_cp("arbitrary" if nc else "parallel", "arbitrary"),
    )(u, wgu8, wd8, *spread)


def ffn_bwd_act(dy, gp, wd8, name):
    S, D = dy.shape
    n = gp.shape[-1]
    T = min(FFN_TILE, S)

    def body(dy_ref, gp_ref, wd_ref, a_ref, dgp_ref):
        da = _dot(dy_ref[...], wd_ref[0], _NT)
        g = gp_ref[0, 0].astype(F32)
        p = gp_ref[1, 0].astype(F32)
        s = 1.0 / (1.0 + jnp.exp(-g))
        sl = g * s
        a_ref[0] = (sl * p).astype(BF16)
        dgp_ref[0, 0] = (da * p * (s * (1.0 + g * (1.0 - s)))).astype(BF16)
        dgp_ref[1, 0] = (da * sl).astype(BF16)

    gp_spec = pl.BlockSpec((2, 1, T, n), lambda i, j: (0, j, i, 0))
    return pl.pallas_call(
        body, name=name, grid=(S // T, N_DEV),
        out_shape=(jax.ShapeDtypeStruct((N_DEV, S, n), BF16), jax.ShapeDtypeStruct((2, N_DEV, S, n), BF16)),
        in_specs=[pl.BlockSpec((T, D), lambda i, j: (i, 0)), gp_spec, pl.BlockSpec((1, n, D), lambda i, j: (j, 0, 0))],
        out_specs=(pl.BlockSpec((1, T, n), lambda i, j: (j, i, 0)), gp_spec), compiler_params=_cp("parallel", "parallel"),
    )(dy, gp, wd8)


def ffn_bwd_x(dgp, wgu8, name, carry=None):
    S, n = dgp.shape[-2:]
    D = wgu8.shape[-2]
    T = min(MM_TILE, S)
    return mm(dgp, wgu8, mode="nt", grid=(S // T, 1, 2 * N_DEV), a_blk=(1, 1, T, n),
              a_map=lambda i, j, k: (k // N_DEV, k % N_DEV, i, 0), b_blk=(1, 1, D, n),
              b_map=lambda i, j, k: (k % N_DEV, k // N_DEV, 0, 0), o_shape=(S, D), o_blk=(T, D),
              o_map=lambda i, j, k: (i, 0), o_dtype=F32, name=name, carry=carry)


def ffn_bwd_wgu(u, dgp, name):
    S, D = u.shape
    n = dgp.shape[-1]
    T = min(MM_TILE, S)
    return mm(u, dgp, mode="tn", grid=(N_DEV, 2, S // T), a_blk=(T, D), a_map=lambda j, t, k: (k, 0),
              b_blk=(1, 1, T, n), b_map=lambda j, t, k: (t, j, k, 0), o_shape=(N_DEV, 2, D, n), o_blk=(1, 1, D, n),
              o_map=lambda j, t, k: (j, t, 0, 0), o_dtype=BF16, name=name)


def ffn_bwd_wd(a8, dy, name):
    n = a8.shape[-1]
    S, D = dy.shape
    T = min(MM_TILE, S)
    return mm(a8, dy, mode="tn", grid=(N_DEV, 1, S // T), a_blk=(1, T, n), a_map=lambda j, _, k: (j, k, 0),
              b_blk=(T, D), b_map=lambda j, _, k: (k, 0), o_shape=(N_DEV, n, D), o_blk=(1, n, D),
              o_map=lambda j, _, k: (j, 0, 0), o_dtype=BF16, name=name)


def adam_update(recv, a_idx, w, m, v, name, layer=None, into=None):
    L, R, C = w.shape
    TR = _row_tile(R, max(8, ADAM_BLOCK // C))
    nsrc = recv.shape[0]

    def body(r_ref, w_ref, m_ref, v_ref, *rest):
        g_ref, d_ref, nm_ref, nv_ref = rest[-4:]
        g = r_ref[0, 0, 0].astype(F32)
        for k in range(1, nsrc):
            g = g + r_ref[k, 0, 0].astype(F32)
        delta, nm, nv = _adam(w_ref[0], g, m_ref[0], v_ref[0])
        g_ref[0], d_ref[0], nm_ref[0], nv_ref[0] = g, delta, nm, nv

    if layer is None:
        grid, wmap, rmap = (L, R // TR), (lambda l, i: (l, i, 0)), (lambda l, i: (0, a_idx, l, i, 0))
    else:
        grid, wmap, rmap = (1, R // TR), (lambda l, i: (layer, i, 0)), (lambda l, i: (0, a_idx, 0, i, 0))
    wspec = pl.BlockSpec((1, TR, C), wmap)
    out = jax.ShapeDtypeStruct(w.shape, F32)
    prior = list(into) if into is not None else []
    return pl.pallas_call(
        body, name=name, grid=grid, out_shape=(out,) * 4,
        in_specs=[pl.BlockSpec((nsrc, 1, 1, TR, C), rmap), wspec, wspec, wspec] + [_HBM] * len(prior),
        out_specs=(wspec,) * 4, input_output_aliases={4 + k: k for k in range(len(prior))},
        compiler_params=_cp("parallel", "parallel"),
    )(recv, w, m, v, *prior)


def sum_devices(x, name):
    _, R, C = x.shape

    def body(x_ref, o_ref):
        s = x_ref[0]
        for k in range(1, N_DEV):
            s = s + x_ref[k]
        o_ref[...] = s

    return pl.pallas_call(
        body, name=name, out_shape=jax.ShapeDtypeStruct((R, C), F32),
        in_specs=[pl.BlockSpec(memory_space=pltpu.VMEM)], out_specs=pl.BlockSpec(memory_space=pltpu.VMEM),
    )(x)


def _pack(parts):
    flat = jnp.concatenate([p.reshape(-1) for p in parts])
    pad = (-flat.shape[0]) % (8 * LANES)
    return jnp.pad(flat, (0, pad)).reshape(-1, LANES)


def _unpack(packed, shapes, lead=()):
    flat = packed.reshape(lead + (-1,))
    out, off = [], 0
    for s in shapes:
        size = 1
        for d in s:
            size *= d
        out.append(flat[..., off:off + size].reshape(lead + tuple(s)))
        off += size
    return out


def kernel(x, c, norm_mix_g, norm_ffn_g, w_mod, b_mod, pool_w, pool_scale, conv_w_in, conv_w, conv_w_out, sb_w_qkv, sb_w_o, ffn_w_gate, ffn_w_up, ffn_w_down, final_g, loss_target, m_norm_mix_g, m_norm_ffn_g, m_w_mod, m_b_mod, m_pool_w, m_pool_scale, m_conv_w_in, m_conv_w, m_conv_w_out, m_sb_w_qkv, m_sb_w_o, m_ffn_w_gate, m_ffn_w_up, m_ffn_w_down, m_final_g, v_norm_mix_g, v_norm_ffn_g, v_w_mod, v_b_mod, v_pool_w, v_pool_scale, v_conv_w_in, v_conv_w, v_conv_w_out, v_sb_w_qkv, v_sb_w_o, v_ffn_w_gate, v_ffn_w_up, v_ffn_w_down, v_final_g):
    S, D = x.shape[1:]
    L = N_LAYERS
    G = len(POOL_WINDOWS)
    C = D // G
    dloc = D // N_DEV
    nmod = w_mod.shape[-1]
    me = _my_index()

    def ffn_local(l):
        return [jnp.stack([ffn_w_gate[l], ffn_w_up[l]]).astype(BF16), ffn_w_down[l].astype(BF16)]

    wgu8 = {0: all_gather(ffn_local(0)[0], "ag_ffn_gate_up_0")}
    wd8 = {0: all_gather(ffn_local(0)[1], "ag_ffn_down_0")}
    wcin8 = all_gather(conv_w_in.astype(BF16), "ag_conv_in")
    wcout = all_gather(conv_w_out.astype(BF16), "ag_conv_out").reshape(D, D)
    wqkv8 = all_gather(sb_w_qkv.astype(BF16), "ag_sb_qkv")
    wo = all_gather(sb_w_o.astype(BF16), "ag_sb_o").reshape(D, D)
    pw = all_gather(pool_w.astype(BF16), "ag_pool_w")
    pw = pw.transpose(1, 2, 0, 3, 4).reshape(pool_w.shape[0], G, C, C)
    small_shapes = [(1, D), (3, dloc), (pool_scale.shape[0], dloc)]
    small = all_gather(_pack([c, conv_w, pool_scale]), "ag_small_in")
    c_all, cw_all, ps_all = _unpack(small, small_shapes, (N_DEV,))
    c_pad = jnp.pad(c_all.reshape(N_DEV, D), ((0, N_DEV), (0, 0)))
    cw = cw_all.transpose(1, 0, 2).reshape(3, 1, D)
    ps = ps_all.transpose(1, 0, 2).reshape(-1, D)

    mod_part = mod_project(c_pad, w_mod, "mod_project")
    mod_all = all_gather(mod_part, "ag_mod")
    mod = lax.dynamic_index_in_dim(mod_all, me, axis=2, keepdims=False)
    mod = mod.transpose(1, 0, 2).reshape(L, N_MOD * D) + b_mod

    def mod_vec(i, k):
        return mod[i, k * D:(k + 1) * D].reshape(1, D)

    h = x[0]
    y_prev = gate_prev = None
    saved = []
    for i in range(L):
        sh_m, sc_m, g_m, sh_f, sc_f, g_f = (mod_vec(i, k) for k in range(N_MOD))
        kind, j = i % 3, i // 3
        gn_m, gn_f = norm_mix_g[i].reshape(1, D), norm_ffn_g[i].reshape(1, D)
        h_mix, u = resid_norm(h, y_prev, gate_prev, gn_m, sc_m, sh_m, F32 if kind == 0 else BF16, f"norm_mix_{i}")
        if kind == 0:
            diff, y = pool_fwd(u, pw[j], ps[j].reshape(1, D), f"pool_fwd_{i}")
            mix = (diff,)
        elif kind == 1:
            U = proj_fwd(u, wcin8, F32, f"conv_in_{i}")
            q = conv_fwd(U, cw, f"conv_fwd_{i}")
            y = dense_fwd(q, wcout, f"conv_out_{i}")
            mix = (u, U, q)
        else:
            qkv = proj_fwd(u, wqkv8, BF16, f"sb_qkv_{i}")
            o, lt, first = attn_fwd(qkv, f"attn_fwd_{i}")
            y = dense_fwd(o, wo, f"sb_out_{i}")
            mix = (u, qkv, o, lt, first)
        h_ffn, u_ffn = resid_norm(h_mix, y, g_m, gn_f, sc_f, sh_f, BF16, f"norm_ffn_{i}")
        nxt = ffn_local(i + 1) if i + 1 < L else []
        gp, y_ffn, *spread = ffn_fwd(u_ffn, wgu8[i], wd8[i], f"ffn_fwd_{i}", spread=nxt)
        if nxt:
            wgu8[i + 1] = gather_pass(spread[0], f"ag_pass_ffn_gate_up_{i + 1}")
            wd8[i + 1] = gather_pass(spread[1], f"ag_pass_ffn_down_{i + 1}")
        saved.append((h_mix, mix, y, h_ffn, u_ffn, gp, y_ffn))
        h, y_prev, gate_prev = h_ffn, y_ffn, g_f

    dh, dfg, loss_part = final_loss(h, y_prev, gate_prev, final_g.reshape(1, D), loss_target[0], "final_loss")
    loss = lax.psum(loss_part[0, 0], ("x", "y", "c"))

    pending, ffn_recv = [], {}
    dpw, dps, dmod, dgn_mix, dgn_ffn = {}, {}, [None] * L, [None] * L, [None] * L
    for i in reversed(range(L)):
        sh_m, sc_m, g_m, sh_f, sc_f, g_f = (mod_vec(i, k) for k in range(N_MOD))
        kind, j = i % 3, i // 3
        gn_m, gn_f = norm_mix_g[i].reshape(1, D), norm_ffn_g[i].reshape(1, D)
        h_mix, mix, y, h_ffn, u_ffn, gp, y_ffn = saved[i]

        dy, dg_f = gate_bwd(dh, y_ffn, g_f, f"gate_bwd_ffn_{i}")
        a8, dgp = ffn_bwd_act(dy, gp, wd8[i], f"ffn_bwd_act_{i}")
        if pending:
            du, *ffn_recv[i + 1] = ffn_bwd_x(dgp, wgu8[i], f"ffn_bwd_x_{i}", carry=("chip", pending))
        else:
            du = ffn_bwd_x(dgp, wgu8[i], f"ffn_bwd_x_{i}")
        pending = []
        for tag, g8 in (("gate_up", ffn_bwd_wgu(u_ffn, dgp, f"ffn_bwd_wgu_{i}")), ("down", ffn_bwd_wd(a8, dy, f"ffn_bwd_wd_{i}"))):
            half = sibling_exchange(g8, f"d2d_ffn_{tag}_{i}")
            pending.append(pair_add(g8, half, f"pair_ffn_{tag}_{i}"))
        dh, dgn_ffn[i], dsc_f, dsh_f = norm_bwd(du, h_ffn, dh, gn_f, sc_f, f"norm_bwd_ffn_{i}")

        dy, dg_m = gate_bwd(dh, y, g_m, f"gate_bwd_mix_{i}")
        if kind == 0:
            (diff,) = mix
            dd, dpw[j], dps[j] = pool_bwd(dy, diff, pw[j], ps[j].reshape(1, D), f"pool_bwd_{i}")
            du = pool_window_bwd(dd, f"pool_window_bwd_{i}")
        elif kind == 1:
            u, U, q = mix
            dq = dense_bwd_x(dy, wcout, f"conv_out_bwd_x_{i}")
            dwcout = dense_bwd_w(q, dy, f"conv_out_bwd_w_{i}")
            dU, dcw = conv_bwd(dq, U, cw, f"conv_bwd_{i}")
            du = proj_bwd_x(dU, wcin8, f"conv_in_bwd_x_{i}")
            dwcin8 = proj_bwd_w(u, dU, f"conv_in_bwd_w_{i}")
        else:
            u, qkv, o, lt, first = mix
            do = dense_bwd_x(dy, wo, f"sb_out_bwd_x_{i}")
            dwo = dense_bwd_w(o, dy, f"sb_out_bwd_w_{i}")
            dq_, dk_, dv_ = attn_bwd(qkv, lt, first, do, f"attn_bwd_{i}")
            dqkv = jnp.concatenate([dq_, dk_, dv_], axis=1)
            du = proj_bwd_x(dqkv, wqkv8, f"sb_qkv_bwd_x_{i}")
            dwqkv8 = proj_bwd_w(u, dqkv, f"sb_qkv_bwd_w_{i}")
        dh, dgn_mix[i], dsc_m, dsh_m = norm_bwd(du, h_mix, dh, gn_m, sc_m, f"norm_bwd_mix_{i}")
        dmod[i] = jnp.concatenate([dsh_m, dsc_m, dg_m, dsh_f, dsc_f, dg_f], axis=1)
    grad_x = dh[None]

    def reduce_update(g8, w, m_, v_, name):
        recv = reduce_exchange(g8, name)
        w3 = w.reshape((-1,) + w.shape[-2:])
        recv = recv.reshape((N_CHIPS, 1) + w3.shape)
        outs = adam_update(recv, 0, w3, m_.reshape(w3.shape), v_.reshape(w3.shape), "adam_" + name)
        return [o_.reshape(w.shape) for o_ in outs]

    ffn_recv[0] = [chip_exchange(t, f"ici_ffn_{tag}_0") for tag, t in zip(("gate_up", "down"), pending)]
    up_gate = up_up = up_down = None
    for l in reversed(range(L)):
        r_gu, r_d = ffn_recv[l]
        r_gu = r_gu.reshape((N_CHIPS, 2, 1) + r_gu.shape[2:])
        r_d = r_d.reshape((N_CHIPS, 1, 1) + r_d.shape[1:])
        up_gate = adam_update(r_gu, 0, ffn_w_gate, m_ffn_w_gate, v_ffn_w_gate, f"adam_ffn_gate_{l}", layer=l, into=up_gate)
        up_up = adam_update(r_gu, 1, ffn_w_up, m_ffn_w_up, v_ffn_w_up, f"adam_ffn_up_{l}", layer=l, into=up_up)
        up_down = adam_update(r_d, 0, ffn_w_down, m_ffn_w_down, v_ffn_w_down, f"adam_ffn_down_{l}", layer=l, into=up_down)
    up_cin = reduce_update(dwcin8, conv_w_in, m_conv_w_in, v_conv_w_in, "conv_in")
    up_cout = reduce_update(dwcout.reshape(N_DEV, 1, dloc, D), conv_w_out, m_conv_w_out, v_conv_w_out, "conv_out")
    up_qkv = reduce_update(dwqkv8, sb_w_qkv, m_sb_w_qkv, v_sb_w_qkv, "sb_qkv")
    up_o = reduce_update(dwo.reshape(N_DEV, 1, dloc, D), sb_w_o, m_sb_w_o, v_sb_w_o, "sb_o")
    dpw_all = jnp.stack([dpw[j] for j in range(pool_w.shape[0])])
    dpw8 = dpw_all.reshape(-1, G, N_DEV, C // N_DEV, C).transpose(2, 0, 1, 3, 4).astype(BF16)
    up_pool = reduce_update(dpw8, pool_w, m_pool_w, v_pool_w, "pool_w")

    dps_all = jnp.concatenate([dps[j] for j in range(pool_scale.shape[0])], axis=0)
    dmod_loc = jnp.concatenate(dmod, axis=0)
    part_shapes = [(L, D), (L, D), (1, D), (L, N_MOD * D), dps_all.shape, (3, 1, D)]
    parts8 = all_gather(_pack([jnp.concatenate(dgn_mix, axis=0), jnp.concatenate(dgn_ffn, axis=0), dfg, dmod_loc,
                               dps_all, dcw]), "ag_small_out")
    g_mix, g_ffn, g_fin, g_bmod, g_ps, g_cw = _unpack(sum_devices(parts8, "sum_small"), part_shapes)
    g_ps = lax.dynamic_slice_in_dim(g_ps, me * dloc, dloc, axis=1)
    g_cw = lax.dynamic_slice_in_dim(g_cw.reshape(3, D), me * dloc, dloc, axis=1)
    dmod_all = _unpack(parts8, part_shapes, (N_DEV,))[3]
    dmod_cols = lax.dynamic_slice_in_dim(dmod_all, me * nmod, nmod, axis=2)
    dmod_pad = jnp.pad(dmod_cols.transpose(1, 0, 2), ((0, 0), (0, N_DEV), (0, 0)))
    up_wmod = mod_grad_adam(c_pad, dmod_pad, w_mod, m_w_mod, v_w_mod, "mod_grad_adam")

    small_w = [norm_mix_g, norm_ffn_g, final_g, b_mod, pool_scale, conv_w]
    small_m = [m_norm_mix_g, m_norm_ffn_g, m_final_g, m_b_mod, m_pool_scale, m_conv_w]
    small_v = [v_norm_mix_g, v_norm_ffn_g, v_final_g, v_b_mod, v_pool_scale, v_conv_w]
    small_g = [g_mix, g_ffn, g_fin.reshape(final_g.shape), g_bmod, g_ps, g_cw.reshape(conv_w.shape)]
    packed = [_pack(t)[None] for t in (small_w, small_m, small_v)]
    up_small = adam_update(_pack(small_g)[None, None, None], 0, *packed, "adam_small")
    sshapes = [t.shape for t in small_w]
    up_small = [_unpack(t[0], sshapes) for t in up_small]
    (s_mix, s_ffn, s_fin, s_bmod, s_ps, s_cw) = zip(*up_small)

    per_weight = [s_mix, s_ffn, up_wmod, s_bmod, up_pool, s_ps, up_cin, s_cw, up_cout, up_qkv, up_o,
                  up_gate, up_up, up_down, s_fin]
    outs = [loss, grad_x]
    for kind_idx in range(4):
        outs.extend(t[kind_idx] for t in per_weight)
    return tuple(outs)
```
